```python
import jax
import jax.numpy as jnp
from jax import lax
import numpy as np

D_MODEL = 4096
BATCH = 4
SEQ = 4096
DEPTH = 1
DEC_BATCH = 16
DEC_SEQ = 16
PAST_LEN = 1024

CHUNK = 64
N_MEM = 256
FOX_HEADS = 16
FOX_HEAD_DIM = 128
FOX_WIDTH = FOX_HEADS * FOX_HEAD_DIM
Q_BLOCK = 128
GLA_HEADS = 4
GLA_WIDTH = D_MODEL - FOX_WIDTH
GLA_KEY_WIDTH = GLA_WIDTH // 2
GLA_DK = GLA_KEY_WIDTH // GLA_HEADS
GLA_DV = GLA_WIDTH // GLA_HEADS
GLA_GATE_RANK = 16
GLA_TAU = 16.0
GLA_BLOCK = 32
MEM_HEADS = 4
MEM_HEAD_DIM = 128
MEM_WIDTH = MEM_HEADS * MEM_HEAD_DIM
PEER_HEADS = 8
PEER_NKEYS = 128
PEER_EXPERTS = PEER_NKEYS * PEER_NKEYS
PEER_QDIM = 256
PEER_HALF = PEER_QDIM // 2
PEER_TOPK = 16
PEER_BLOCK = 64
IN_WIDTH = 3 * FOX_WIDTH + FOX_HEADS + 2 * GLA_KEY_WIDTH + 2 * GLA_WIDTH + GLA_GATE_RANK
EPS = 1e-6
NEG_INF = -1e30

kernel_name = 'hybrid_fox_gla_peer_stream_step'


def rmsnorm(x, g):
    xf = x.astype(jnp.float32)
    y = xf * lax.rsqrt(jnp.mean(xf * xf, axis=-1, keepdims=True) + EPS)
    return (y * g.astype(jnp.float32)).astype(x.dtype)


def _mixer_inputs(n, w_in, b_fgate, w_gla_a2, b_gla_a):
    B, S, _ = n.shape
    z = jnp.einsum('bsd,de->bse', n, w_in)
    sizes = (FOX_WIDTH, FOX_WIDTH, FOX_WIDTH, FOX_HEADS,
             GLA_KEY_WIDTH, GLA_KEY_WIDTH, GLA_WIDTH, GLA_WIDTH, GLA_GATE_RANK)
    parts, o = [], 0
    for sz in sizes:
        parts.append(z[..., o:o + sz])
        o += sz
    fq, fk, fv, ff, gq, gk, gv, gg, ga = parts
    fshape = (B, S, FOX_HEADS, FOX_HEAD_DIM)
    logf = jax.nn.log_sigmoid(ff.astype(jnp.float32) + b_fgate.astype(jnp.float32))
    a = jnp.einsum('bsr,rk->bsk', ga, w_gla_a2).astype(jnp.float32) + b_gla_a.astype(jnp.float32)
    lg = (jax.nn.log_sigmoid(a) / GLA_TAU).reshape(B, S, GLA_HEADS, GLA_DK)
    gq = gq.reshape(B, S, GLA_HEADS, GLA_DK).astype(jnp.float32) * (GLA_DK ** -0.5)
    gk = gk.reshape(B, S, GLA_HEADS, GLA_DK).astype(jnp.float32)
    gv = gv.reshape(B, S, GLA_HEADS, GLA_DV).astype(jnp.float32)
    return (fq.reshape(fshape), fk.reshape(fshape), fv.reshape(fshape), logf,
            gq, gk, gv, gg, lg)


def _fox_prompt(q, k, v, logf):
    S = q.shape[1]
    c = jnp.cumsum(logf, axis=1).transpose(0, 2, 1)
    scale = FOX_HEAD_DIM ** -0.5
    outs = []
    for i in range(S // Q_BLOCK):
        lo, hi = i * Q_BLOCK, (i + 1) * Q_BLOCK
        s = jnp.einsum('bqhd,bkhd->bhqk', q[:, lo:hi], k[:, :hi],
                       preferred_element_type=jnp.float32) * scale
        s = s + c[:, :, lo:hi, None] - c[:, :, None, :hi]
        mask = (lo + jnp.arange(Q_BLOCK))[:, None] >= jnp.arange(hi)[None, :]
        p = jax.nn.softmax(jnp.where(mask, s, NEG_INF), axis=-1)
        outs.append(jnp.einsum('bhqk,bkhd->bqhd', p.astype(v.dtype), v[:, :hi]))
    return jnp.concatenate(outs, axis=1)


def _fox_sample(q, k, v, logf, ck, cv, clogf):
    P, L = ck.shape[1], q.shape[1]
    kk = jnp.concatenate([ck.astype(k.dtype), k], axis=1)
    vv = jnp.concatenate([cv.astype(v.dtype), v], axis=1)
    c = jnp.cumsum(jnp.concatenate([clogf.astype(jnp.float32), logf], axis=1), axis=1)
    c = c.transpose(0, 2, 1)
    s = jnp.einsum('bqhd,bkhd->bhqk', q, kk, preferred_element_type=jnp.float32) * (FOX_HEAD_DIM ** -0.5)
    s = s + c[:, :, P:, None] - c[:, :, None, :]
    mask = (P + jnp.arange(L))[:, None] >= jnp.arange(P + L)[None, :]
    p = jax.nn.softmax(jnp.where(mask, s, NEG_INF), axis=-1)
    return jnp.einsum('bhqk,bkhd->bqhd', p.astype(vv.dtype), vv)


def _gla_block(q, k, v, lg, S0):
    L = q.shape[1]
    b = jnp.cumsum(lg, axis=1)
    qt = q * jnp.exp(b)
    kt = k * jnp.exp(-b)
    causal = jnp.tril(jnp.ones((L, L), dtype=bool))
    A = jnp.where(causal, jnp.einsum('blhk,bmhk->bhlm', qt, kt), 0.0)
    o = jnp.einsum('blhk,bhkv->blhv', qt, S0) + jnp.einsum('bhlm,bmhv->blhv', A, v)
    b_last = b[:, -1]
    S1 = (jnp.exp(b_last)[..., None] * S0
          + jnp.einsum('blhk,blhv->bhkv', k * jnp.exp(b_last[:, None] - b), v))
    return o, S1


def _gla_prompt(q, k, v, lg):
    B, S, H, _ = q.shape
    nb = S // GLA_BLOCK

    def to_blocks(t):
        return jnp.moveaxis(t.reshape(B, nb, GLA_BLOCK, H, t.shape[-1]), 1, 0)

    def step(state, xs):
        qb, kb, vb, gb = xs
        o, s_new = _gla_block(qb, kb, vb, gb, state)
        return s_new, o

    S0 = jnp.zeros((B, H, GLA_DK, GLA_DV), jnp.float32)
    s_end, o = lax.scan(step, S0, (to_blocks(q), to_blocks(k), to_blocks(v), to_blocks(lg)))
    o = jnp.moveaxis(o, 0, 1).reshape(B, S, H, GLA_DV)
    return o, s_end


def _mix_out(fo, go, gg, norm_gla_out, w_out):
    B, S = fo.shape[:2]
    dt = gg.dtype
    go = rmsnorm(go, norm_gla_out).reshape(B, S, GLA_WIDTH).astype(dt) * jax.nn.silu(gg)
    cat = jnp.concatenate([fo.reshape(B, S, FOX_WIDTH).astype(dt), go], axis=-1)
    return jnp.einsum('bse,ed->bsd', cat, w_out)


def _mem_kv(mem, g, w_mk, w_mv):
    B, N, _ = mem.shape
    m = rmsnorm(mem, g)
    k = jnp.einsum('bnd,de->bne', m, w_mk).reshape(B, N, MEM_HEADS, MEM_HEAD_DIM)
    v = jnp.einsum('bnd,de->bne', m, w_mv).reshape(B, N, MEM_HEADS, MEM_HEAD_DIM)
    return k, v


def _mem_attend(n, mk, mv, w_mq, w_mo):
    B, S, _ = n.shape
    q = jnp.einsum('bsd,de->bse', n, w_mq).reshape(B, S, MEM_HEADS, MEM_HEAD_DIM)
    s = jnp.einsum('bshd,bnhd->bhsn', q, mk.astype(q.dtype),
                   preferred_element_type=jnp.float32) * (MEM_HEAD_DIM ** -0.5)
    p = jax.nn.softmax(s, axis=-1)
    o = jnp.einsum('bhsn,bnhd->bshd', p.astype(q.dtype), mv.astype(q.dtype))
    return jnp.einsum('bse,ed->bsd', o.reshape(B, S, MEM_WIDTH), w_mo)


def _peer(n, w_pq, sub_keys, expert_u, expert_v):
    B, S, D = n.shape
    T = B * S
    E = PEER_HEADS * PEER_TOPK
    q = jnp.einsum('bsd,de->bse', n, w_pq).reshape(T, PEER_HEADS, 2, PEER_HALF)
    sc = jnp.einsum('thpc,pkc->thpk', q, sub_keys, preferred_element_type=jnp.float32)
    s1, i1 = lax.top_k(sc[:, :, 0], PEER_TOPK)
    s2, i2 = lax.top_k(sc[:, :, 1], PEER_TOPK)
    cand = (s1[..., :, None] + s2[..., None, :]).reshape(T, PEER_HEADS, PEER_TOPK * PEER_TOPK)
    best, j = lax.top_k(cand, PEER_TOPK)
    e1 = jnp.take_along_axis(i1, j // PEER_TOPK, axis=-1)
    e2 = jnp.take_along_axis(i2, j % PEER_TOPK, axis=-1)
    idx = (e1 * PEER_NKEYS + e2).reshape(T, E)
    gate = jax.nn.softmax(best, axis=-1).reshape(T, E)
    pad = (-T) % PEER_BLOCK
    xt = jnp.pad(n.reshape(T, D), ((0, pad), (0, 0)))
    idx = jnp.pad(idx, ((0, pad), (0, 0)))
    gate = jnp.pad(gate, ((0, pad), (0, 0)))
    nb = (T + pad) // PEER_BLOCK

    def block(args):
        xb, ib, gb = args
        h = jnp.einsum('ted,td->te', expert_u[ib], xb, preferred_element_type=jnp.float32)
        a = (jax.nn.gelu(h) * gb).astype(xb.dtype)
        return jnp.einsum('te,ted->td', a, expert_v[ib])

    out = lax.map(block, (xt.reshape(nb, PEER_BLOCK, D),
                          idx.reshape(nb, PEER_BLOCK, E),
                          gate.reshape(nb, PEER_BLOCK, E)))
    return out.reshape(nb * PEER_BLOCK, D)[:T].reshape(B, S, D)


def _layer(h, mem_k, mem_v, past, norm_mix, w_in, b_fgate, w_gla_a2, b_gla_a, norm_gla_out,
           w_out, norm_mem_q, w_mq, w_mo, norm_ffn, w_pq, sub_keys, expert_u, expert_v):
    n = rmsnorm(h, norm_mix)
    fq, fk, fv, logf, gq, gk, gv, gg, lg = _mixer_inputs(n, w_in, b_fgate, w_gla_a2, b_gla_a)
    if past is None:
        fo = _fox_prompt(fq, fk, fv, logf)
        go, gla_state = _gla_prompt(gq, gk, gv, lg)
    else:
        ck, cv, clogf, s0 = past
        fo = _fox_sample(fq, fk, fv, logf, ck, cv, clogf)
        go, gla_state = _gla_block(gq, gk, gv, lg, s0.astype(jnp.float32))
    h = h + _mix_out(fo, go, gg, norm_gla_out, w_out)
    h = h + _mem_attend(rmsnorm(h, norm_mem_q), mem_k, mem_v, w_mq, w_mo)
    h = h + _peer(rmsnorm(h, norm_ffn), w_pq, sub_keys, expert_u, expert_v)
    return h, fk, fv, logf, gla_state


def _normal(key, shape, scale=1.0):
    return scale * jax.random.normal(key, shape, jnp.float32)


def setup_inputs(seed: int = 0) -> dict:
    key = jax.random.key(seed)
    ks = jax.random.split(key, 28)
    L = DEPTH
    return {
        'x_prompt': _normal(ks[0], (BATCH, SEQ, D_MODEL)),
        'x_sample': _normal(ks[1], (DEC_BATCH, DEC_SEQ, D_MODEL)),
        'cache_fox_k': _normal(ks[2], (L, DEC_BATCH, PAST_LEN, FOX_HEADS, FOX_HEAD_DIM)),
        'cache_fox_v': _normal(ks[3], (L, DEC_BATCH, PAST_LEN, FOX_HEADS, FOX_HEAD_DIM)),
        'cache_fox_logf': jax.nn.log_sigmoid(2.0 + _normal(ks[4], (L, DEC_BATCH, PAST_LEN, FOX_HEADS))),
        'state_gla': _normal(ks[5], (L, DEC_BATCH, GLA_HEADS, GLA_DK, GLA_DV), 0.5),
        'cache_mem_k': _normal(ks[6], (L, DEC_BATCH, N_MEM, MEM_HEADS, MEM_HEAD_DIM)),
        'cache_mem_v': _normal(ks[7], (L, DEC_BATCH, N_MEM, MEM_HEADS, MEM_HEAD_DIM)),
        'mem_prompt': _normal(ks[8], (BATCH, N_MEM, D_MODEL)),
        'norm_mix': 1.0 + _normal(ks[9], (L, D_MODEL), 0.01),
        'w_in': _normal(ks[10], (L, D_MODEL, IN_WIDTH), D_MODEL ** -0.5),
        'b_fgate': 2.0 + _normal(ks[11], (L, FOX_HEADS), 0.1),
        'w_gla_a2': _normal(ks[12], (L, GLA_GATE_RANK, GLA_KEY_WIDTH), GLA_GATE_RANK ** -0.5),
        'b_gla_a': _normal(ks[13], (L, GLA_KEY_WIDTH), 0.1),
        'norm_gla_out': 1.0 + _normal(ks[14], (L, GLA_DV), 0.01),
        'w_out': _normal(ks[15], (L, D_MODEL, D_MODEL), D_MODEL ** -0.5),
        'norm_mem_q': 1.0 + _normal(ks[16], (L, D_MODEL), 0.01),
        'norm_mem_kv': 1.0 + _normal(ks[17], (L, D_MODEL), 0.01),
        'w_mq': _normal(ks[18], (L, D_MODEL, MEM_WIDTH), D_MODEL ** -0.5),
        'w_mk': _normal(ks[19], (L, D_MODEL, MEM_WIDTH), D_MODEL ** -0.5),
        'w_mv': _normal(ks[20], (L, D_MODEL, MEM_WIDTH), D_MODEL ** -0.5),
        'w_mo': _normal(ks[21], (L, MEM_WIDTH, D_MODEL), MEM_WIDTH ** -0.5),
        'norm_ffn': 1.0 + _normal(ks[22], (L, D_MODEL), 0.01),
        'w_pq': _normal(ks[23], (L, D_MODEL, PEER_HEADS * PEER_QDIM), D_MODEL ** -0.5),
        'sub_keys': _normal(ks[24], (L, 2, PEER_NKEYS, PEER_HALF), PEER_HALF ** -0.5),
        'expert_u': _normal(ks[25], (L, PEER_EXPERTS, D_MODEL), D_MODEL ** -0.5),
        'expert_v': _normal(ks[26], (L, PEER_EXPERTS, D_MODEL), 0.25),
        'norm_final': 1.0 + _normal(ks[27], (D_MODEL,), 0.01),
    }


def reference(x_prompt, x_sample, cache_fox_k, cache_fox_v, cache_fox_logf, state_gla,
              cache_mem_k, cache_mem_v, mem_prompt, norm_mix, w_in, b_fgate, w_gla_a2, b_gla_a,
              norm_gla_out, w_out, norm_mem_q, norm_mem_kv, w_mq, w_mk, w_mv, w_mo, norm_ffn,
              w_pq, sub_keys, expert_u, expert_v, norm_final):
    hp, hs = x_prompt, x_sample
    fkp, fvp, flp, gsp, mkp, mvp = [], [], [], [], [], []
    fks, fvs, fls, gss = [], [], [], []
    for l in range(DEPTH):
        shared = (norm_mix[l], w_in[l], b_fgate[l], w_gla_a2[l], b_gla_a[l], norm_gla_out[l],
                  w_out[l], norm_mem_q[l], w_mq[l], w_mo[l], norm_ffn[l], w_pq[l], sub_keys[l],
                  expert_u[l], expert_v[l])
        mk, mv = _mem_kv(mem_prompt, norm_mem_kv[l], w_mk[l], w_mv[l])
        hp, fk, fv, lf, gs = _layer(hp, mk, mv, None, *shared)
        fkp.append(fk); fvp.append(fv); flp.append(lf); gsp.append(gs)
        mkp.append(mk); mvp.append(mv)
        past = (cache_fox_k[l], cache_fox_v[l], cache_fox_logf[l], state_gla[l])
        hs, fk, fv, lf, gs = _layer(hs, cache_mem_k[l], cache_mem_v[l], past, *shared)
        fks.append(fk); fvs.append(fv); fls.append(lf); gss.append(gs)
    y_prompt = rmsnorm(hp, norm_final)
    y_sample = rmsnorm(hs, norm_final)
    return (y_prompt, y_sample,
            jnp.stack(fkp), jnp.stack(fvp), jnp.stack(flp), jnp.stack(gsp),
            jnp.stack(mkp), jnp.stack(mvp),
            jnp.stack(fks), jnp.stack(fvs), jnp.stack(fls), jnp.stack(gss))
```

```python
import functools

import jax
import jax.numpy as jnp
from jax import lax
from jax.experimental import pallas as pl
from jax.experimental.pallas import tpu as pltpu

F32 = jnp.float32
BF16 = jnp.bfloat16

EPS = 1e-6
NEG_INF = -1e30

FOX_HEADS = 16
FOX_HEAD_DIM = 128
FOX_WIDTH = FOX_HEADS * FOX_HEAD_DIM
GLA_HEADS = 4
GLA_DK = 256
GLA_DV = 512
GLA_KEY_WIDTH = GLA_HEADS * GLA_DK
GLA_WIDTH = GLA_HEADS * GLA_DV
GLA_GATE_RANK = 16
GLA_TAU = 16.0
GLA_CHUNK = 128
GLA_SUB = 32
MEM_HEADS = 4
MEM_HEAD_DIM = 128
MEM_WIDTH = MEM_HEADS * MEM_HEAD_DIM
PEER_HEADS = 8
PEER_NKEYS = 128
PEER_HALF = 128
PEER_TOPK = 16
LANES = 128
MAIN_WIDTH = 3 * FOX_WIDTH + 2 * GLA_KEY_WIDTH + 2 * GLA_WIDTH

COL_FQ, COL_FK, COL_FV = 0, FOX_WIDTH, 2 * FOX_WIDTH
COL_GQ = 3 * FOX_WIDTH
COL_GK = COL_GQ + GLA_KEY_WIDTH
COL_GV = COL_GK + GLA_KEY_WIDTH
COL_GG = COL_GV + GLA_WIDTH

VMEM_LIMIT_BYTES = 52 * 2**20


def _pick(n, prefs):
    for p in prefs:
        if n % p == 0:
            return p
    raise ValueError(f"no tile in {prefs} divides {n}")


def _params(*sem):
    return pltpu.CompilerParams(dimension_semantics=sem, vmem_limit_bytes=VMEM_LIMIT_BYTES)


def _log_sigmoid(x):
    return -(jnp.maximum(-x, 0.0) + jnp.log1p(jnp.exp(-jnp.abs(x))))


def _split3(x):
    hi = x.astype(BF16)
    r1 = x - hi.astype(F32)
    mid = r1.astype(BF16)
    lo = (r1 - mid.astype(F32)).astype(BF16)
    return hi, mid, lo


def _rmsnorm_kernel(x_ref, g_ref, o_ref):
    x = x_ref[...]
    y = x * lax.rsqrt(jnp.mean(x * x, axis=-1, keepdims=True) + EPS)
    o_ref[...] = (y * g_ref[...]).astype(o_ref.dtype)


def _rmsnorm(x, g):
    t, d = x.shape
    tm = _pick(t, (512, 256, 128))
    return pl.pallas_call(
        _rmsnorm_kernel,
        grid=(t // tm,),
        in_specs=[pl.BlockSpec((tm, d), lambda i: (i, 0)), pl.BlockSpec((1, d), lambda i: (0, 0))],
        out_specs=pl.BlockSpec((tm, d), lambda i: (i, 0)),
        out_shape=jax.ShapeDtypeStruct((t, d), BF16),
        compiler_params=_params("parallel"),
        name="rmsnorm",
    )(x, g.reshape(1, d))


def _mm_kernel(a_ref, w_ref, *o_refs):
    r = jnp.dot(a_ref[...], w_ref[...], preferred_element_type=F32)
    for o_ref in o_refs:
        o_ref[...] = r.astype(o_ref.dtype)


def _mm(a, w, out_dtypes):
    t, k = a.shape
    n = w.shape[1]
    tm = _pick(t, (512, 256, 128))
    tn = _pick(n, (1024, 512, 256, 128))
    outs = pl.pallas_call(
        _mm_kernel,
        grid=(n // tn, t // tm),
        in_specs=[pl.BlockSpec((tm, k), lambda j, i: (i, 0)), pl.BlockSpec((k, tn), lambda j, i: (0, j))],
        out_specs=[pl.BlockSpec((tm, tn), lambda j, i: (i, j)) for _ in out_dtypes],
        out_shape=[jax.ShapeDtypeStruct((t, n), dt) for dt in out_dtypes],
        compiler_params=_params("parallel", "parallel"),
        name="proj",
    )(a, w)
    return outs


def _small_kernel(n_ref, w_ref, b_ref, o_ref):
    z = jnp.dot(n_ref[...], w_ref[...], preferred_element_type=F32)
    col = lax.broadcasted_iota(jnp.int32, z.shape, 1)
    o_ref[...] = jnp.where(col < FOX_HEADS, _log_sigmoid(z + b_ref[...]), z)


def _small_proj(n, w_small, b_fgate_pad):
    t, k = n.shape
    tm = _pick(t, (512, 256, 128))
    return pl.pallas_call(
        _small_kernel,
        grid=(t // tm,),
        in_specs=[pl.BlockSpec((tm, k), lambda i: (i, 0)), pl.BlockSpec((k, LANES), lambda i: (0, 0)),
                  pl.BlockSpec((1, LANES), lambda i: (0, 0))],
        out_specs=pl.BlockSpec((tm, LANES), lambda i: (i, 0)),
        out_shape=jax.ShapeDtypeStruct((t, LANES), F32),
        compiler_params=_params("parallel"),
        name="small_proj",
    )(n, w_small, b_fgate_pad)


def _cumsum_kernel(x_ref, o_ref, carry_ref, *, lb):
    @pl.when(pl.program_id(1) == 0)
    def _():
        carry_ref[...] = jnp.zeros_like(carry_ref)

    xt = x_ref[0].T
    r = lax.broadcasted_iota(jnp.int32, (lb, lb), 0)
    c = lax.broadcasted_iota(jnp.int32, (lb, lb), 1)
    tri = (r <= c).astype(BF16)
    acc = carry_ref[...]
    for part in _split3(xt):
        acc = acc + jnp.dot(part, tri, preferred_element_type=F32)
    o_ref[0] = acc
    carry_ref[...] = jnp.broadcast_to(acc[:, lb - 1:lb], carry_ref.shape)


def _cumsum_t(x, lb):
    b, l, _ = x.shape
    return pl.pallas_call(
        functools.partial(_cumsum_kernel, lb=lb),
        grid=(b, l // lb),
        in_specs=[pl.BlockSpec((1, lb, LANES), lambda i, j: (i, j, 0))],
        out_specs=pl.BlockSpec((1, LANES, lb), lambda i, j: (i, 0, j)),
        out_shape=jax.ShapeDtypeStruct((b, LANES, l), F32),
        scratch_shapes=[pltpu.VMEM((LANES, lb), F32)],
        compiler_params=_params("parallel", "arbitrary"),
        name="cumsum_logf",
    )(x)


def _fox_prompt_kernel(q_ref, k_ref, v_ref, c_ref, o_ref, m_sc, l_sc, acc_sc, *, tq):
    qi = pl.program_id(2)
    q = q_ref[...]
    scale = FOX_HEAD_DIM ** -0.5
    m_sc[...] = jnp.full_like(m_sc, NEG_INF)
    l_sc[...] = jnp.zeros_like(l_sc)
    acc_sc[...] = jnp.zeros_like(acc_sc)

    def chunk(kc, masked):
        ks = pl.multiple_of(kc * tq, tq)
        k = k_ref[pl.ds(ks, tq), :]
        v = v_ref[pl.ds(ks, tq), :]
        s = lax.dot_general(q, k, (((1,), (1,)), ((), ())), preferred_element_type=F32) * scale
        s = s - c_ref[0, kc]
        if masked:
            r = lax.broadcasted_iota(jnp.int32, s.shape, 0)
            c = lax.broadcasted_iota(jnp.int32, s.shape, 1)
            s = jnp.where(r >= c, s, NEG_INF)
        m_prev = m_sc[...]
        m_new = jnp.maximum(m_prev, jnp.max(s, axis=1, keepdims=True))
        alpha = jnp.exp(m_prev - m_new)
        p = jnp.exp(s - m_new)
        l_sc[...] = alpha * l_sc[...] + jnp.sum(p, axis=1, keepdims=True)
        acc_sc[...] = alpha * acc_sc[...] + jnp.dot(p.astype(BF16), v, preferred_element_type=F32)
        m_sc[...] = m_new

    def body(kc, carry):
        chunk(kc, False)
        return carry

    lax.fori_loop(0, qi, body, 0)
    chunk(qi, True)
    o_ref[...] = (acc_sc[...] / l_sc[...]).astype(o_ref.dtype)


def _fox_prompt(z16, c4, batch, seq):
    tq = _pick(seq, (512, 256, 128))
    nq = seq // tq
    kb, vb = COL_FK // FOX_HEAD_DIM, COL_FV // FOX_HEAD_DIM
    return pl.pallas_call(
        functools.partial(_fox_prompt_kernel, tq=tq),
        grid=(batch, FOX_HEADS, nq),
        in_specs=[
            pl.BlockSpec((tq, FOX_HEAD_DIM), lambda b, h, i: (b * nq + i, h)),
            pl.BlockSpec((seq, FOX_HEAD_DIM), lambda b, h, i: (b, kb + h)),
            pl.BlockSpec((seq, FOX_HEAD_DIM), lambda b, h, i: (b, vb + h)),
            pl.BlockSpec((1, nq, 1, tq), lambda b, h, i: (b * LANES + h, 0, 0, 0)),
        ],
        out_specs=pl.BlockSpec((tq, FOX_HEAD_DIM), lambda b, h, i: (b * nq + i, h)),
        out_shape=jax.ShapeDtypeStruct((batch * seq, FOX_WIDTH), BF16),
        scratch_shapes=[pltpu.VMEM((tq, 1), F32), pltpu.VMEM((tq, 1), F32), pltpu.VMEM((tq, FOX_HEAD_DIM), F32)],
        compiler_params=_params("parallel", "parallel", "arbitrary"),
        name="fox_prompt",
    )(z16, z16, z16, c4)


def _fox_sample_kernel(q_ref, kn_ref, vn_ref, ck_ref, cv_ref, c_ref, o_ref, *, past, new):
    scale = FOX_HEAD_DIM ** -0.5
    q = q_ref[...]
    pad = jnp.zeros((LANES - new, FOX_HEAD_DIM), BF16)
    kn = jnp.concatenate([kn_ref[...], pad], axis=0)
    vn = jnp.concatenate([vn_ref[...], pad], axis=0)
    dn = (((1,), (1,)), ((), ()))
    c = c_ref[0]
    s1 = lax.dot_general(q, ck_ref[...].astype(BF16), dn, preferred_element_type=F32) * scale - c[:, :past]
    s2 = lax.dot_general(q, kn, dn, preferred_element_type=F32) * scale - c[:, past:]
    r = lax.broadcasted_iota(jnp.int32, s2.shape, 0)
    cc = lax.broadcasted_iota(jnp.int32, s2.shape, 1)
    s2 = jnp.where(r >= cc, s2, NEG_INF)
    m = jnp.maximum(jnp.max(s1, axis=1, keepdims=True), jnp.max(s2, axis=1, keepdims=True))
    p1 = jnp.exp(s1 - m)
    p2 = jnp.exp(s2 - m)
    l = jnp.sum(p1, axis=1, keepdims=True) + jnp.sum(p2, axis=1, keepdims=True)
    o = (jnp.dot(p1.astype(BF16), cv_ref[...].astype(BF16), preferred_element_type=F32)
         + jnp.dot(p2.astype(BF16), vn, preferred_element_type=F32))
    o_ref[...] = (o / l).astype(o_ref.dtype)


def _fox_sample(z16, ck, cv, c3, batch, new, past):
    kb, vb = COL_FK // FOX_HEAD_DIM, COL_FV // FOX_HEAD_DIM
    return pl.pallas_call(
        functools.partial(_fox_sample_kernel, past=past, new=new),
        grid=(batch, FOX_HEADS),
        in_specs=[
            pl.BlockSpec((new, FOX_HEAD_DIM), lambda b, h: (b, h)),
            pl.BlockSpec((new, FOX_HEAD_DIM), lambda b, h: (b, kb + h)),
            pl.BlockSpec((new, FOX_HEAD_DIM), lambda b, h: (b, vb + h)),
            pl.BlockSpec((past, FOX_HEAD_DIM), lambda b, h: (b, h)),
            pl.BlockSpec((past, FOX_HEAD_DIM), lambda b, h: (b, h)),
            pl.BlockSpec((1, 1, past + LANES), lambda b, h: (b * LANES + h, 0, 0)),
        ],
        out_specs=pl.BlockSpec((new, FOX_HEAD_DIM), lambda b, h: (b, h)),
        out_shape=jax.ShapeDtypeStruct((batch * new, FOX_WIDTH), BF16),
        compiler_params=_params("parallel", "parallel"),
        name="fox_sample",
    )(z16, z16, z16, ck, cv, c3)


def _gla_kernel(q_ref, k_ref, v_ref, gg_ref, sm_ref, wa_ref, ba_ref, g_ref, s0_ref, go_ref, sout_ref, s_sc,
                *, valid, has_init):
    ci = pl.program_id(2)
    C = GLA_CHUNK

    @pl.when(ci == 0)
    def _():
        if has_init:
            s_sc[...] = s0_ref[0, 0]
        else:
            s_sc[...] = jnp.zeros_like(s_sc)

    def rows(ref, dtype):
        x = ref[...].astype(dtype)
        if valid < C:
            x = jnp.concatenate([x, jnp.zeros((C - valid, x.shape[1]), dtype)], axis=0)
        return x

    row = lax.broadcasted_iota(jnp.int32, (C, C), 0)
    col = lax.broadcasted_iota(jnp.int32, (C, C), 1)

    a = jnp.dot(rows(sm_ref, BF16), wa_ref[...], preferred_element_type=F32) + ba_ref[...]
    lg = _log_sigmoid(a) * (1.0 / GLA_TAU)
    if valid < C:
        lg = jnp.where(lax.broadcasted_iota(jnp.int32, lg.shape, 0) < valid, lg, 0.0)
    low = (row >= col).astype(BF16)
    bc = jnp.zeros_like(lg)
    for part in _split3(lg):
        bc = bc + jnp.dot(low, part, preferred_element_type=F32)

    q = rows(q_ref, F32) * (GLA_DK ** -0.5)
    k = rows(k_ref, F32)
    v = rows(v_ref, BF16)
    s = s_sc[...]

    o_state = jnp.dot((q * jnp.exp(bc)).astype(BF16), s.astype(BF16), preferred_element_type=F32)
    krow = lax.broadcasted_iota(jnp.int32, (C, GLA_DK), 0)
    sub_row = lax.broadcasted_iota(jnp.int32, (GLA_SUB, C), 0)
    sub_col = lax.broadcasted_iota(jnp.int32, (GLA_SUB, C), 1)
    outs = []
    for i in range(C // GLA_SUB):
        r0, r1 = i * GLA_SUB, (i + 1) * GLA_SUB
        if valid <= r0:
            outs.append(o_state[r0:r1])
            continue
        base = bc[r0 - 1:r0] if i > 0 else jnp.zeros((1, GLA_DK), F32)
        qt = (q[r0:r1] * jnp.exp(bc[r0:r1] - base)).astype(BF16)
        kt = jnp.where(krow < r1, k * jnp.exp(base - bc), 0.0).astype(BF16)
        att = lax.dot_general(qt, kt, (((1,), (1,)), ((), ())), preferred_element_type=F32)
        att = jnp.where(sub_col <= sub_row + r0, att, 0.0)
        outs.append(o_state[r0:r1] + jnp.dot(att.astype(BF16), v, preferred_element_type=F32))
    o = jnp.concatenate(outs, axis=0)

    b_end = bc[C - 1:C]
    kd = k * jnp.exp(b_end - bc)
    decay = jnp.broadcast_to(jnp.exp(b_end), (LANES, GLA_DK)).T
    decay = jnp.concatenate([decay] * (GLA_DV // LANES), axis=1)
    s_new = decay * s + jnp.dot(kd.T.astype(BF16), v, preferred_element_type=F32)
    s_sc[...] = s_new

    @pl.when(ci == pl.num_programs(2) - 1)
    def _():
        sout_ref[0, 0] = s_new

    o = o[:valid]
    on = o * lax.rsqrt(jnp.mean(o * o, axis=-1, keepdims=True) + EPS) * g_ref[...]
    gg = gg_ref[...]
    go_ref[...] = (on * (gg * jax.nn.sigmoid(gg))).astype(go_ref.dtype)


def _gla(z32, z16, small, wa_pad, b_a, g_out, s0, batch, tokens):
    if tokens >= GLA_CHUNK:
        assert tokens % GLA_CHUNK == 0
        blk, nch = GLA_CHUNK, tokens // GLA_CHUNK
    else:
        blk, nch = tokens, 1
    has_init = s0 is not None
    if s0 is None:
        s0 = jnp.zeros((1, 1, GLA_DK, GLA_DV), F32)
        s0_map = lambda b, h, c: (0, 0, 0, 0)
    else:
        s0_map = lambda b, h, c: (b, h, 0, 0)
    qb, kb = COL_GQ // GLA_DK, COL_GK // GLA_DK
    vb, gb = COL_GV // GLA_DV, COL_GG // GLA_DV
    go, s_out = pl.pallas_call(
        functools.partial(_gla_kernel, valid=blk, has_init=has_init),
        grid=(batch, GLA_HEADS, nch),
        in_specs=[
            pl.BlockSpec((blk, GLA_DK), lambda b, h, c: (b * nch + c, qb + h)),
            pl.BlockSpec((blk, GLA_DK), lambda b, h, c: (b * nch + c, kb + h)),
            pl.BlockSpec((blk, GLA_DV), lambda b, h, c: (b * nch + c, vb + h)),
            pl.BlockSpec((blk, GLA_DV), lambda b, h, c: (b * nch + c, gb + h)),
            pl.BlockSpec((blk, LANES), lambda b, h, c: (b * nch + c, 0)),
            pl.BlockSpec((LANES, GLA_DK), lambda b, h, c: (0, h)),
            pl.BlockSpec((1, GLA_DK), lambda b, h, c: (0, h)),
            pl.BlockSpec((1, GLA_DV), lambda b, h, c: (0, 0)),
            pl.BlockSpec((1, 1, GLA_DK, GLA_DV), s0_map),
        ],
        out_specs=[
            pl.BlockSpec((blk, GLA_DV), lambda b, h, c: (b * nch + c, h)),
            pl.BlockSpec((1, 1, GLA_DK, GLA_DV), lambda b, h, c: (b, h, 0, 0)),
        ],
        out_shape=[jax.ShapeDtypeStruct((batch * tokens, GLA_WIDTH), BF16),
                   jax.ShapeDtypeStruct((batch, GLA_HEADS, GLA_DK, GLA_DV), F32)],
        scratch_shapes=[pltpu.VMEM((GLA_DK, GLA_DV), F32)],
        compiler_params=_params("parallel", "parallel", "arbitrary"),
        name="gla",
    )(z32, z32, z16, z32, small, wa_pad, b_a, g_out, s0)
    return go, s_out


def _outproj_kernel(fo_ref, go_ref, wt_ref, wb_ref, x_ref, o_ref):
    o_ref[...] = (x_ref[...] + jnp.dot(fo_ref[...], wt_ref[...], preferred_element_type=F32)
                  + jnp.dot(go_ref[...], wb_ref[...], preferred_element_type=F32))


def _outproj(fo, go, w_out, x):
    t, d = x.shape
    half = fo.shape[1]
    tm = _pick(t, (512, 256, 128))
    tn = _pick(d, (1024, 512))
    return pl.pallas_call(
        _outproj_kernel,
        grid=(d // tn, t // tm),
        in_specs=[
            pl.BlockSpec((tm, half), lambda j, i: (i, 0)),
            pl.BlockSpec((tm, half), lambda j, i: (i, 0)),
            pl.BlockSpec((half, tn), lambda j, i: (0, j)),
            pl.BlockSpec((half, tn), lambda j, i: (1, j)),
            pl.BlockSpec((tm, tn), lambda j, i: (i, j)),
        ],
        out_specs=pl.BlockSpec((tm, tn), lambda j, i: (i, j)),
        out_shape=jax.ShapeDtypeStruct((t, d), F32),
        compiler_params=_params("parallel", "parallel"),
        name="outproj",
    )(fo, go, w_out, w_out, x)


def _mem_kernel(h_ref, gq_ref, wq_ref, mk_ref, mv_ref, wo_ref, gf_ref, h2_ref, n3t_ref, *, per_batch, n_mem):
    h = h_ref[...]
    tm = h.shape[0]
    n2 = (h * lax.rsqrt(jnp.mean(h * h, axis=-1, keepdims=True) + EPS) * gq_ref[...]).astype(BF16)
    q = jnp.dot(n2, wq_ref[...], preferred_element_type=F32)
    scale = MEM_HEAD_DIM ** -0.5
    rows_per = min(per_batch, tm)
    parts = []
    for b in range(tm // rows_per):
        heads = []
        for hd in range(MEM_HEADS):
            cs = slice(hd * MEM_HEAD_DIM, (hd + 1) * MEM_HEAD_DIM)
            qh = q[b * rows_per:(b + 1) * rows_per, cs].astype(BF16)
            kh = mk_ref[b * n_mem:(b + 1) * n_mem, cs].astype(BF16)
            vh = mv_ref[b * n_mem:(b + 1) * n_mem, cs].astype(BF16)
            s = lax.dot_general(qh, kh, (((1,), (1,)), ((), ())), preferred_element_type=F32) * scale
            s = s - jnp.max(s, axis=1, keepdims=True)
            p = jnp.exp(s)
            p = p / jnp.sum(p, axis=1, keepdims=True)
            heads.append(jnp.dot(p.astype(BF16), vh, preferred_element_type=F32))
        parts.append(jnp.concatenate(heads, axis=1))
    o = jnp.concatenate(parts, axis=0).astype(BF16)
    h2 = h + jnp.dot(o, wo_ref[...], preferred_element_type=F32)
    h2_ref[...] = h2
    n3 = h2 * lax.rsqrt(jnp.mean(h2 * h2, axis=-1, keepdims=True) + EPS) * gf_ref[...]
    n3t_ref[...] = n3.T.astype(BF16)


def _mem_block(h1, g_q, w_mq, mk, mv, w_mo, g_ffn, per_batch, n_mem):
    t, d = h1.shape
    tm = _pick(t, (256, 128)) if per_batch >= 256 else LANES
    nb = max(tm // per_batch, 1)
    tiles_per_batch = max(per_batch // tm, 1)
    return pl.pallas_call(
        functools.partial(_mem_kernel, per_batch=per_batch, n_mem=n_mem),
        grid=(t // tm,),
        in_specs=[
            pl.BlockSpec((tm, d), lambda i: (i, 0)),
            pl.BlockSpec((1, d), lambda i: (0, 0)),
            pl.BlockSpec((d, MEM_WIDTH), lambda i: (0, 0)),
            pl.BlockSpec((nb * n_mem, MEM_WIDTH), lambda i: (i // tiles_per_batch, 0)),
            pl.BlockSpec((nb * n_mem, MEM_WIDTH), lambda i: (i // tiles_per_batch, 0)),
            pl.BlockSpec((MEM_WIDTH, d), lambda i: (0, 0)),
            pl.BlockSpec((1, d), lambda i: (0, 0)),
        ],
        out_specs=[pl.BlockSpec((tm, d), lambda i: (i, 0)), pl.BlockSpec((d, tm), lambda i: (0, i))],
        out_shape=[jax.ShapeDtypeStruct((t, d), F32), jax.ShapeDtypeStruct((d, t), BF16)],
        compiler_params=_params("parallel"),
        name="mem_attn",
    )(h1, g_q, w_mq, mk, mv, w_mo, g_ffn)


def _extract_top(vals, key, count):
    cur = vals
    rank = jnp.full(vals.shape, float(count), F32)
    tops = []
    big = jnp.float32(2**30)
    for r in range(count):
        m = jnp.max(cur, axis=0, keepdims=True)
        first = jnp.min(jnp.where(cur == m, key, big), axis=0, keepdims=True)
        hit = key == first
        rank = jnp.where(hit, float(r), rank)
        cur = jnp.where(hit, -jnp.inf, cur)
        tops.append(m)
    return tops, rank


def _peer_route_kernel(w_ref, n3t_ref, keys_ref, p_ref):
    K = PEER_TOPK
    qp = jnp.dot(w_ref[...], n3t_ref[...], preferred_element_type=F32)
    s1 = jnp.dot(keys_ref[0], qp[:PEER_HALF].astype(BF16), preferred_element_type=F32)
    s2 = jnp.dot(keys_ref[1], qp[PEER_HALF:].astype(BF16), preferred_element_type=F32)
    tm = s1.shape[1]
    kiota = lax.broadcasted_iota(jnp.int32, s1.shape, 0).astype(F32)
    a, rank1 = _extract_top(s1, kiota, K)
    b, rank2 = _extract_top(s2, kiota, K)

    bmat = jnp.concatenate(b, axis=0)
    jio = lax.broadcasted_iota(jnp.int32, (K, tm), 0).astype(F32)
    blocks, keys = [], []
    half = K // 2
    for i in range(half):
        blocks.append(jnp.where(jio < float(K // (i + 1)), a[i] + bmat, -jnp.inf))
        keys.append(jio + float(i * K))
    blocks.append(jnp.concatenate([a[i] + b[0] for i in range(half, K)], axis=0))
    keys.append((lax.broadcasted_iota(jnp.int32, (K - half, tm), 0).astype(F32) + float(half)) * float(K))
    cand = jnp.concatenate(blocks, axis=0)
    ckey = jnp.concatenate(keys, axis=0)
    _, crank = _extract_top(cand, ckey, K)
    sel = crank < float(K)
    z = jnp.sum(jnp.where(sel, jnp.exp(cand - (a[0] + b[0])), 0.0), axis=0, keepdims=True)
    self32 = sel.astype(F32)
    counts = [jnp.sum(self32[i * K:(i + 1) * K], axis=0, keepdims=True) for i in range(half)]
    counts += [self32[half * K + i:half * K + i + 1] for i in range(K - half)]

    cnt = jnp.zeros(s1.shape, F32)
    for i in range(K):
        cnt = jnp.where(rank1 == float(i), counts[i], cnt)
    p_ref[0, 0] = rank2
    p_ref[0, 1] = jnp.exp(s2 - b[0])
    p_ref[0, 2] = cnt
    p_ref[0, 3] = jnp.exp(s1 - a[0]) / z


def _peer_route(w_pq_t, n3t, keys):
    d, t = n3t.shape
    tm = _pick(t, (512, 256, 128))
    qd = 2 * PEER_HALF
    return pl.pallas_call(
        _peer_route_kernel,
        grid=(t // tm, PEER_HEADS),
        in_specs=[
            pl.BlockSpec((qd, d), lambda i, h: (h, 0)),
            pl.BlockSpec((d, tm), lambda i, h: (0, i)),
            pl.BlockSpec((2, PEER_NKEYS, PEER_HALF), lambda i, h: (0, 0, 0)),
        ],
        out_specs=pl.BlockSpec((1, 4, PEER_NKEYS, tm), lambda i, h: (h, 0, 0, i)),
        out_shape=jax.ShapeDtypeStruct((PEER_HEADS, 4, PEER_NKEYS, t), F32),
        compiler_params=_params("parallel", "parallel"),
        name="peer_route",
    )(w_pq_t, n3t, keys)


def _gelu_tanh(x):
    return 0.5 * x * (1.0 + jnp.tanh(0.7978845608028654 * (x + 0.044715 * (x * x * x))))


def _peer_kernel(n3t_ref, p_ref, u_ref, vt_ref, o_ref, *, ec):
    e = pl.program_id(1)

    @pl.when(e == 0)
    def _():
        o_ref[...] = jnp.zeros_like(o_ref)

    n3t = n3t_ref[...]
    acts = []
    for kk in range(ec // PEER_NKEYS):
        k1 = e * (ec // PEER_NKEYS) + kk
        h = jnp.dot(u_ref[kk * PEER_NKEYS:(kk + 1) * PEER_NKEYS, :], n3t, preferred_element_type=F32)
        gate = jnp.zeros_like(h)
        for hd in range(PEER_HEADS):
            cnt = p_ref[hd, 2, pl.ds(k1, 1), :]
            e1 = p_ref[hd, 3, pl.ds(k1, 1), :]
            gate = gate + jnp.where(p_ref[hd, 0] < cnt, p_ref[hd, 1], 0.0) * e1
        acts.append((_gelu_tanh(h) * gate).astype(BF16))
    act = jnp.concatenate(acts, axis=0)
    o_ref[...] += jnp.dot(vt_ref[...], act, preferred_element_type=F32)


def _peer(n3t, route, u, vt):
    d, t = n3t.shape
    n_exp = u.shape[0]
    tm = _pick(t, (512, 256, 128))
    ec = 512
    return pl.pallas_call(
        functools.partial(_peer_kernel, ec=ec),
        grid=(t // tm, n_exp // ec),
        in_specs=[
            pl.BlockSpec((d, tm), lambda i, e: (0, i), pipeline_mode=pl.Buffered(1)),
            pl.BlockSpec((PEER_HEADS, 4, PEER_NKEYS, tm), lambda i, e: (0, 0, 0, i), pipeline_mode=pl.Buffered(1)),
            pl.BlockSpec((ec, d), lambda i, e: (e, 0)),
            pl.BlockSpec((d, ec), lambda i, e: (0, e)),
        ],
        out_specs=pl.BlockSpec((d, tm), lambda i, e: (0, i)),
        out_shape=jax.ShapeDtypeStruct((d, t), F32),
        compiler_params=_params("parallel", "arbitrary"),
        name="peer_experts",
    )(n3t, route, u, vt)


def _final_kernel(h_ref, pt_ref, g_ref, o_ref):
    x = h_ref[...] + pt_ref[...].T
    y = x * lax.rsqrt(jnp.mean(x * x, axis=-1, keepdims=True) + EPS)
    o_ref[...] = y * g_ref[...]


def _final(h2, peer_t, g):
    t, d = h2.shape
    tm = _pick(t, (256, 128))
    return pl.pallas_call(
        _final_kernel,
        grid=(t // tm,),
        in_specs=[pl.BlockSpec((tm, d), lambda i: (i, 0)), pl.BlockSpec((d, tm), lambda i: (0, i)),
                  pl.BlockSpec((1, d), lambda i: (0, 0))],
        out_specs=pl.BlockSpec((tm, d), lambda i: (i, 0)),
        out_shape=jax.ShapeDtypeStruct((t, d), F32),
        compiler_params=_params("parallel"),
        name="final_norm",
    )(h2, peer_t, g)


def _layer(x, w, mem_k, mem_v, batch, tokens, n_mem, past):
    n = _rmsnorm(x, w["norm_mix"])
    z32, z16 = _mm(n, w["w_main"], (F32, BF16))
    small = _small_proj(n, w["w_small"], w["b_fgate"])
    fk = z32[:, COL_FK:COL_FK + FOX_WIDTH]
    fv = z32[:, COL_FV:COL_FV + FOX_WIDTH]
    logf = small[:, :FOX_HEADS]

    if past is None:
        lb = _pick(tokens, (512, 256, 128))
        c = _cumsum_t(small.reshape(batch, tokens, LANES), lb)
        tq = _pick(tokens, (512, 256, 128))
        c4 = c.reshape(batch * LANES, tokens // tq, 1, tq)
        fo = _fox_prompt(z16, c4, batch, tokens)
        go, gla_state = _gla(z32, z16, small, w["wa_pad"], w["b_gla_a"], w["norm_gla_out"], None, batch, tokens)
    else:
        ck, cv, clogf, s0 = past
        plen = ck.shape[1]
        lf = jnp.concatenate([
            jnp.pad(clogf.astype(F32), ((0, 0), (0, 0), (0, LANES - FOX_HEADS))),
            small.reshape(batch, tokens, LANES),
            jnp.zeros((batch, LANES - tokens, LANES), F32)], axis=1)
        c = _cumsum_t(lf, LANES)
        c3 = c.reshape(batch * LANES, 1, plen + LANES)
        fo = _fox_sample(z16, ck.reshape(batch * plen, FOX_WIDTH), cv.reshape(batch * plen, FOX_WIDTH), c3,
                         batch, tokens, plen)
        go, gla_state = _gla(z32, z16, small, w["wa_pad"], w["b_gla_a"], w["norm_gla_out"], s0.astype(F32),
                             batch, tokens)

    h1 = _outproj(fo, go, w["w_out"], x)
    h2, n3t = _mem_block(h1, w["norm_mem_q"], w["w_mq"], mem_k, mem_v, w["w_mo"], w["norm_ffn"], tokens, n_mem)
    route = _peer_route(w["w_pq_t"], n3t, w["sub_keys"])
    peer_t = _peer(n3t, route, w["expert_u"], w["expert_v_t"])
    return h2, peer_t, fk, fv, logf, gla_state


def kernel(x_prompt, x_sample, cache_fox_k, cache_fox_v, cache_fox_logf, state_gla, cache_mem_k, cache_mem_v, mem_prompt, norm_mix, w_in, b_fgate, w_gla_a2, b_gla_a, norm_gla_out, w_out, norm_mem_q, norm_mem_kv, w_mq, w_mk, w_mv, w_mo, norm_ffn, w_pq, sub_keys, expert_u, expert_v, norm_final):
    depth = w_in.shape[0]
    assert depth == 1, "one trunk layer"
    bp, sp, d = x_prompt.shape
    bs, ss, _ = x_sample.shape
    n_mem = mem_prompt.shape[1]
    l = 0

    wi = w_in[l]
    o_ff = 3 * FOX_WIDTH
    o_g = o_ff + FOX_HEADS
    o_ga = o_g + 2 * GLA_KEY_WIDTH + 2 * GLA_WIDTH
    w = {
        "norm_mix": norm_mix[l],
        "w_main": jnp.concatenate([wi[:, :o_ff], wi[:, o_g:o_ga]], axis=1).astype(BF16),
        "w_small": jnp.concatenate([wi[:, o_ff:o_g], wi[:, o_ga:],
                                    jnp.zeros((d, LANES - FOX_HEADS - GLA_GATE_RANK), F32)], axis=1).astype(BF16),
        "b_fgate": jnp.pad(b_fgate[l], (0, LANES - FOX_HEADS)).reshape(1, LANES),
        "wa_pad": jnp.pad(w_gla_a2[l], ((FOX_HEADS, LANES - FOX_HEADS - GLA_GATE_RANK), (0, 0))).astype(BF16),
        "b_gla_a": b_gla_a[l].reshape(1, GLA_KEY_WIDTH),
        "norm_gla_out": norm_gla_out[l].reshape(1, GLA_DV),
        "w_out": w_out[l].astype(BF16),
        "norm_mem_q": norm_mem_q[l].reshape(1, d),
        "w_mq": w_mq[l].astype(BF16),
        "w_mo": w_mo[l].astype(BF16),
        "norm_ffn": norm_ffn[l].reshape(1, d),
        "w_pq_t": w_pq[l].T.astype(BF16),
        "sub_keys": sub_keys[l].astype(BF16),
        "expert_u": expert_u[l].astype(BF16),
        "expert_v_t": expert_v[l].T.astype(BF16),
    }

    m = _rmsnorm(mem_prompt.reshape(bp * n_mem, d), norm_mem_kv[l])
    (mk,) = _mm(m, w_mk[l].astype(BF16), (F32,))
    (mv,) = _mm(m, w_mv[l].astype(BF16), (F32,))

    hp, pp, fkp, fvp, lfp, gsp = _layer(x_prompt.reshape(bp * sp, d), w, mk, mv, bp, sp, n_mem, None)
    past = (cache_fox_k[l], cache_fox_v[l], cache_fox_logf[l], state_gla[l])
    hs, ps, fks, fvs, lfs, gss = _layer(x_sample.reshape(bs * ss, d), w,
                                        cache_mem_k[l].reshape(bs * n_mem, MEM_WIDTH),
                                        cache_mem_v[l].reshape(bs * n_mem, MEM_WIDTH), bs, ss, n_mem, past)

    g_fin = norm_final.reshape(1, d)
    y_prompt = _final(hp, pp, g_fin).reshape(bp, sp, d)
    y_sample = _final(hs, ps, g_fin).reshape(bs, ss, d)
    hshape_p = (1, bp, sp, FOX_HEADS, FOX_HEAD_DIM)
    hshape_s = (1, bs, ss, FOX_HEADS, FOX_HEAD_DIM)
    return (y_prompt, y_sample,
            fkp.reshape(hshape_p), fvp.reshape(hshape_p), lfp.reshape(1, bp, sp, FOX_HEADS), gsp[None],
            mk.reshape(1, bp, n_mem, MEM_HEADS, MEM_HEAD_DIM), mv.reshape(1, bp, n_mem, MEM_HEADS, MEM_HEAD_DIM),
            fks.reshape(hshape_s), fvs.reshape(hshape_s), lfs.reshape(1, bs, ss, FOX_HEADS), gss[None])
```

```python
import functools

import jax
import jax.numpy as jnp
from jax import lax
from jax.experimental import pallas as pl
from jax.experimental.pallas import tpu as pltpu

F32 = jnp.float32
BF16 = jnp.bfloat16

EPS = 1e-6
NEG_INF = -1e30

FOX_HEADS = 16
FOX_HEAD_DIM = 128
FOX_WIDTH = FOX_HEADS * FOX_HEAD_DIM
FOX_HEADS_PER_STEP = 4
LOG2E = 1.4426950408889634
GLA_HEADS = 4
GLA_DK = 256
GLA_DV = 512
GLA_KEY_WIDTH = GLA_HEADS * GLA_DK
GLA_WIDTH = GLA_HEADS * GLA_DV
GLA_GATE_RANK = 16
GLA_TAU = 16.0
GLA_CHUNK = 128
GLA_SUB = 32
MEM_HEADS = 4
MEM_HEAD_DIM = 128
MEM_WIDTH = MEM_HEADS * MEM_HEAD_DIM
PEER_HEADS = 8
PEER_NKEYS = 128
PEER_HALF = 128
PEER_TOPK = 16
PEER_EXPERT_CHUNK = 512
LANES = 128
MAIN_WIDTH = 3 * FOX_WIDTH + 2 * GLA_KEY_WIDTH + 2 * GLA_WIDTH

COL_FQ, COL_FK, COL_FV = 0, FOX_WIDTH, 2 * FOX_WIDTH
COL_GQ = 3 * FOX_WIDTH
COL_GK = COL_GQ + GLA_KEY_WIDTH
COL_GV = COL_GK + GLA_KEY_WIDTH
COL_GG = COL_GV + GLA_WIDTH

VMEM_LIMIT_BYTES = 52 * 2**20


def _pick(n, prefs):
    for p in prefs:
        if n % p == 0:
            return p
    raise ValueError(f"no tile in {prefs} divides {n}")


def _params(*sem):
    return pltpu.CompilerParams(dimension_semantics=sem, vmem_limit_bytes=VMEM_LIMIT_BYTES)


def _log_sigmoid(x):
    return -(jnp.maximum(-x, 0.0) + jnp.log1p(jnp.exp(-jnp.abs(x))))


def _split3(x):
    hi = x.astype(BF16)
    r1 = x - hi.astype(F32)
    mid = r1.astype(BF16)
    lo = (r1 - mid.astype(F32)).astype(BF16)
    return hi, mid, lo


def _rmsnorm_kernel(x_ref, g_ref, o_ref):
    x = x_ref[...]
    y = x * lax.rsqrt(jnp.mean(x * x, axis=-1, keepdims=True) + EPS)
    o_ref[...] = (y * g_ref[...]).astype(o_ref.dtype)


def _rmsnorm(x, g):
    t, d = x.shape
    tm = _pick(t, (512, 256, 128))
    return pl.pallas_call(
        _rmsnorm_kernel,
        grid=(t // tm,),
        in_specs=[pl.BlockSpec((tm, d), lambda i: (i, 0)), pl.BlockSpec((1, d), lambda i: (0, 0))],
        out_specs=pl.BlockSpec((tm, d), lambda i: (i, 0)),
        out_shape=jax.ShapeDtypeStruct((t, d), BF16),
        compiler_params=_params("parallel"),
        name="rmsnorm",
    )(x, g.reshape(1, d))


def _mm_kernel(a_ref, w_ref, *o_refs):
    r = jnp.dot(a_ref[...], w_ref[...], preferred_element_type=F32)
    for o_ref in o_refs:
        o_ref[...] = r.astype(o_ref.dtype)


def _mm(a, w, out_dtypes):
    t, k = a.shape
    n = w.shape[1]
    tm = _pick(t, (512, 256, 128))
    tn = _pick(n, (1024, 512, 256, 128))
    outs = pl.pallas_call(
        _mm_kernel,
        grid=(n // tn, t // tm),
        in_specs=[pl.BlockSpec((tm, k), lambda j, i: (i, 0)), pl.BlockSpec((k, tn), lambda j, i: (0, j))],
        out_specs=[pl.BlockSpec((tm, tn), lambda j, i: (i, j)) for _ in out_dtypes],
        out_shape=[jax.ShapeDtypeStruct((t, n), dt) for dt in out_dtypes],
        compiler_params=_params("parallel", "parallel"),
        name="proj",
    )(a, w)
    return outs


def _small_kernel(n_ref, w_ref, b_ref, o_ref):
    z = jnp.dot(n_ref[...], w_ref[...], preferred_element_type=F32)
    col = lax.broadcasted_iota(jnp.int32, z.shape, 1)
    o_ref[...] = jnp.where(col < FOX_HEADS, _log_sigmoid(z + b_ref[...]), z)


def _small_proj(n, w_small, b_fgate_pad):
    t, k = n.shape
    tm = _pick(t, (512, 256, 128))
    return pl.pallas_call(
        _small_kernel,
        grid=(t // tm,),
        in_specs=[pl.BlockSpec((tm, k), lambda i: (i, 0)), pl.BlockSpec((k, LANES), lambda i: (0, 0)),
                  pl.BlockSpec((1, LANES), lambda i: (0, 0))],
        out_specs=pl.BlockSpec((tm, LANES), lambda i: (i, 0)),
        out_shape=jax.ShapeDtypeStruct((t, LANES), F32),
        compiler_params=_params("parallel"),
        name="small_proj",
    )(n, w_small, b_fgate_pad)


def _cumsum_kernel(x_ref, o_ref, carry_ref, *, lb):
    @pl.when(pl.program_id(1) == 0)
    def _():
        carry_ref[...] = jnp.zeros_like(carry_ref)

    xt = x_ref[0].T
    r = lax.broadcasted_iota(jnp.int32, (lb, lb), 0)
    c = lax.broadcasted_iota(jnp.int32, (lb, lb), 1)
    tri = (r <= c).astype(BF16)
    acc = carry_ref[...]
    for part in _split3(xt):
        acc = acc + jnp.dot(part, tri, preferred_element_type=F32)
    o_ref[0] = acc
    carry_ref[...] = jnp.broadcast_to(acc[:, lb - 1:lb], carry_ref.shape)


def _cumsum_t(x, lb):
    b, l, _ = x.shape
    return pl.pallas_call(
        functools.partial(_cumsum_kernel, lb=lb),
        grid=(b, l // lb),
        in_specs=[pl.BlockSpec((1, lb, LANES), lambda i, j: (i, j, 0))],
        out_specs=pl.BlockSpec((1, LANES, lb), lambda i, j: (i, 0, j)),
        out_shape=jax.ShapeDtypeStruct((b, LANES, l), F32),
        scratch_shapes=[pltpu.VMEM((LANES, lb), F32)],
        compiler_params=_params("parallel", "arbitrary"),
        name="cumsum_logf",
    )(x)


def _fox_prompt_kernel(q_ref, k_ref, v_ref, c_ref, o_ref, m_sc, acc_sc, *, tq, hb):
    qi = pl.program_id(2)
    qscale = FOX_HEAD_DIM ** -0.5 * LOG2E
    m_sc[...] = jnp.full_like(m_sc, NEG_INF)
    acc_sc[...] = jnp.zeros_like(acc_sc)

    def chunk(kc, masked):
        ks = pl.multiple_of(kc * tq, tq)
        ones = jnp.ones((tq, LANES), BF16)
        for hh in range(hb):
            cs = slice(hh * FOX_HEAD_DIM, (hh + 1) * FOX_HEAD_DIM)
            s = lax.dot_general(q_ref[:, cs], k_ref[pl.ds(ks, tq), cs], (((1,), (1,)), ((), ())),
                                preferred_element_type=F32) * qscale - c_ref[hh, kc] * LOG2E
            if masked:
                r = lax.broadcasted_iota(jnp.int32, s.shape, 0)
                c = lax.broadcasted_iota(jnp.int32, s.shape, 1)
                s = jnp.where(r >= c, s, NEG_INF)
            m_prev = m_sc[hh]
            m_new = jnp.maximum(m_prev, jnp.max(s, axis=1, keepdims=True))
            alpha = jnp.exp2(m_prev - m_new)
            p = jnp.exp2(s - jnp.concatenate([m_new] * (tq // LANES), axis=1))
            v_ext = jnp.concatenate([v_ref[pl.ds(ks, tq), cs], ones], axis=1)
            pv = jnp.dot(p.astype(BF16), v_ext, preferred_element_type=F32)
            acc_sc[hh] = jnp.concatenate([alpha, alpha], axis=1) * acc_sc[hh] + pv
            m_sc[hh] = m_new

    def body(kc, carry):
        chunk(kc, False)
        return carry

    lax.fori_loop(0, qi, body, 0)
    chunk(qi, True)
    for hh in range(hb):
        acc = acc_sc[hh]
        o_ref[:, hh * FOX_HEAD_DIM:(hh + 1) * FOX_HEAD_DIM] = (
            acc[:, :FOX_HEAD_DIM] / acc[:, FOX_HEAD_DIM:]).astype(o_ref.dtype)


def _fox_prompt(z16, c4, batch, seq):
    tq = _pick(seq, (512, 256, 128))
    nq = seq // tq
    hb = FOX_HEADS_PER_STEP
    wid = hb * FOX_HEAD_DIM
    kb, vb = COL_FK // wid, COL_FV // wid
    return pl.pallas_call(
        functools.partial(_fox_prompt_kernel, tq=tq, hb=hb),
        grid=(batch, FOX_HEADS // hb, nq),
        in_specs=[
            pl.BlockSpec((tq, wid), lambda b, h, i: (b * nq + i, h)),
            pl.BlockSpec((seq, wid), lambda b, h, i: (b, kb + h)),
            pl.BlockSpec((seq, wid), lambda b, h, i: (b, vb + h)),
            pl.BlockSpec((hb, nq, 1, tq), lambda b, h, i: (b * (LANES // hb) + h, 0, 0, 0)),
        ],
        out_specs=pl.BlockSpec((tq, wid), lambda b, h, i: (b * nq + i, h)),
        out_shape=jax.ShapeDtypeStruct((batch * seq, FOX_WIDTH), BF16),
        scratch_shapes=[pltpu.VMEM((hb, tq, LANES), F32), pltpu.VMEM((hb, tq, 2 * FOX_HEAD_DIM), F32)],
        compiler_params=_params("parallel", "parallel", "arbitrary"),
        name="fox_prompt",
    )(z16, z16, z16, c4)


def _fox_sample_kernel(q_ref, kn_ref, vn_ref, ck_ref, cv_ref, c_ref, o_ref, *, past, new):
    scale = FOX_HEAD_DIM ** -0.5
    q = q_ref[...]
    pad = jnp.zeros((LANES - new, FOX_HEAD_DIM), BF16)
    kn = jnp.concatenate([kn_ref[...], pad], axis=0)
    vn = jnp.concatenate([vn_ref[...], pad], axis=0)
    dn = (((1,), (1,)), ((), ()))
    c = c_ref[0]
    s1 = lax.dot_general(q, ck_ref[...].astype(BF16), dn, preferred_element_type=F32) * scale - c[:, :past]
    s2 = lax.dot_general(q, kn, dn, preferred_element_type=F32) * scale - c[:, past:]
    r = lax.broadcasted_iota(jnp.int32, s2.shape, 0)
    cc = lax.broadcasted_iota(jnp.int32, s2.shape, 1)
    s2 = jnp.where(r >= cc, s2, NEG_INF)
    m = jnp.maximum(jnp.max(s1, axis=1, keepdims=True), jnp.max(s2, axis=1, keepdims=True))
    p1 = jnp.exp(s1 - m)
    p2 = jnp.exp(s2 - m)
    l = jnp.sum(p1, axis=1, keepdims=True) + jnp.sum(p2, axis=1, keepdims=True)
    o = (jnp.dot(p1.astype(BF16), cv_ref[...].astype(BF16), preferred_element_type=F32)
         + jnp.dot(p2.astype(BF16), vn, preferred_element_type=F32))
    o_ref[...] = (o / l).astype(o_ref.dtype)


def _fox_sample(z16, ck, cv, c3, batch, new, past):
    kb, vb = COL_FK // FOX_HEAD_DIM, COL_FV // FOX_HEAD_DIM
    return pl.pallas_call(
        functools.partial(_fox_sample_kernel, past=past, new=new),
        grid=(batch, FOX_HEADS),
        in_specs=[
            pl.BlockSpec((new, FOX_HEAD_DIM), lambda b, h: (b, h)),
            pl.BlockSpec((new, FOX_HEAD_DIM), lambda b, h: (b, kb + h)),
            pl.BlockSpec((new, FOX_HEAD_DIM), lambda b, h: (b, vb + h)),
            pl.BlockSpec((past, FOX_HEAD_DIM), lambda b, h: (b, h)),
            pl.BlockSpec((past, FOX_HEAD_DIM), lambda b, h: (b, h)),
            pl.BlockSpec((1, 1, past + LANES), lambda b, h: (b * LANES + h, 0, 0)),
        ],
        out_specs=pl.BlockSpec((new, FOX_HEAD_DIM), lambda b, h: (b, h)),
        out_shape=jax.ShapeDtypeStruct((batch * new, FOX_WIDTH), BF16),
        compiler_params=_params("parallel", "parallel"),
        name="fox_sample",
    )(z16, z16, z16, ck, cv, c3)


def _gla_kernel(q_ref, k_ref, v_ref, gg_ref, sm_ref, wa_ref, ba_ref, g_ref, s0_ref, go_ref, sout_ref, s_sc,
                *, valid, has_init):
    ci = pl.program_id(2)
    C = GLA_CHUNK

    @pl.when(ci == 0)
    def _():
        if has_init:
            s_sc[...] = s0_ref[0, 0]
        else:
            s_sc[...] = jnp.zeros_like(s_sc)

    def rows(ref, dtype):
        x = ref[...].astype(dtype)
        if valid < C:
            x = jnp.concatenate([x, jnp.zeros((C - valid, x.shape[1]), dtype)], axis=0)
        return x

    row = lax.broadcasted_iota(jnp.int32, (C, C), 0)
    col = lax.broadcasted_iota(jnp.int32, (C, C), 1)

    a = jnp.dot(rows(sm_ref, BF16), wa_ref[...], preferred_element_type=F32) + ba_ref[...]
    lg = _log_sigmoid(a) * (1.0 / GLA_TAU)
    if valid < C:
        lg = jnp.where(lax.broadcasted_iota(jnp.int32, lg.shape, 0) < valid, lg, 0.0)
    low = (row >= col).astype(BF16)
    bc = jnp.zeros_like(lg)
    for part in _split3(lg):
        bc = bc + jnp.dot(low, part, preferred_element_type=F32)

    q = rows(q_ref, F32) * (GLA_DK ** -0.5)
    k = rows(k_ref, F32)
    v = rows(v_ref, BF16)
    s = s_sc[...]

    o_state = jnp.dot((q * jnp.exp(bc)).astype(BF16), s.astype(BF16), preferred_element_type=F32)
    krow = lax.broadcasted_iota(jnp.int32, (C, GLA_DK), 0)
    sub_row = lax.broadcasted_iota(jnp.int32, (GLA_SUB, C), 0)
    sub_col = lax.broadcasted_iota(jnp.int32, (GLA_SUB, C), 1)
    outs = []
    for i in range(C // GLA_SUB):
        r0, r1 = i * GLA_SUB, (i + 1) * GLA_SUB
        if valid <= r0:
            outs.append(o_state[r0:r1])
            continue
        base = bc[r0 - 1:r0] if i > 0 else jnp.zeros((1, GLA_DK), F32)
        qt = (q[r0:r1] * jnp.exp(bc[r0:r1] - base)).astype(BF16)
        kt = jnp.where(krow < r1, k * jnp.exp(base - bc), 0.0).astype(BF16)
        att = lax.dot_general(qt, kt, (((1,), (1,)), ((), ())), preferred_element_type=F32)
        att = jnp.where(sub_col <= sub_row + r0, att, 0.0)
        outs.append(o_state[r0:r1] + jnp.dot(att.astype(BF16), v, preferred_element_type=F32))
    o = jnp.concatenate(outs, axis=0)

    b_end = bc[C - 1:C]
    kd = k * jnp.exp(b_end - bc)
    decay = jnp.broadcast_to(jnp.exp(b_end), (LANES, GLA_DK)).T
    decay = jnp.concatenate([decay] * (GLA_DV // LANES), axis=1)
    s_new = decay * s + jnp.dot(kd.T.astype(BF16), v, preferred_element_type=F32)
    s_sc[...] = s_new

    @pl.when(ci == pl.num_programs(2) - 1)
    def _():
        sout_ref[0, 0] = s_new

    o = o[:valid]
    on = o * lax.rsqrt(jnp.mean(o * o, axis=-1, keepdims=True) + EPS) * g_ref[...]
    gg = gg_ref[...]
    go_ref[...] = (on * (gg * jax.nn.sigmoid(gg))).astype(go_ref.dtype)


def _gla(z32, z16, small, wa_pad, b_a, g_out, s0, batch, tokens):
    if tokens >= GLA_CHUNK:
        assert tokens % GLA_CHUNK == 0
        blk, nch = GLA_CHUNK, tokens // GLA_CHUNK
    else:
        blk, nch = tokens, 1
    has_init = s0 is not None
    if s0 is None:
        s0 = jnp.zeros((1, 1, GLA_DK, GLA_DV), F32)
        s0_map = lambda b, h, c: (0, 0, 0, 0)
    else:
        s0_map = lambda b, h, c: (b, h, 0, 0)
    qb, kb = COL_GQ // GLA_DK, COL_GK // GLA_DK
    vb, gb = COL_GV // GLA_DV, COL_GG // GLA_DV
    go, s_out = pl.pallas_call(
        functools.partial(_gla_kernel, valid=blk, has_init=has_init),
        grid=(batch, GLA_HEADS, nch),
        in_specs=[
            pl.BlockSpec((blk, GLA_DK), lambda b, h, c: (b * nch + c, qb + h)),
            pl.BlockSpec((blk, GLA_DK), lambda b, h, c: (b * nch + c, kb + h)),
            pl.BlockSpec((blk, GLA_DV), lambda b, h, c: (b * nch + c, vb + h)),
            pl.BlockSpec((blk, GLA_DV), lambda b, h, c: (b * nch + c, gb + h)),
            pl.BlockSpec((blk, LANES), lambda b, h, c: (b * nch + c, 0)),
            pl.BlockSpec((LANES, GLA_DK), lambda b, h, c: (0, h)),
            pl.BlockSpec((1, GLA_DK), lambda b, h, c: (0, h)),
            pl.BlockSpec((1, GLA_DV), lambda b, h, c: (0, 0)),
            pl.BlockSpec((1, 1, GLA_DK, GLA_DV), s0_map),
        ],
        out_specs=[
            pl.BlockSpec((blk, GLA_DV), lambda b, h, c: (b * nch + c, h)),
            pl.BlockSpec((1, 1, GLA_DK, GLA_DV), lambda b, h, c: (b, h, 0, 0)),
        ],
        out_shape=[jax.ShapeDtypeStruct((batch * tokens, GLA_WIDTH), BF16),
                   jax.ShapeDtypeStruct((batch, GLA_HEADS, GLA_DK, GLA_DV), F32)],
        scratch_shapes=[pltpu.VMEM((GLA_DK, GLA_DV), F32)],
        compiler_params=_params("parallel", "parallel", "arbitrary"),
        name="gla",
    )(z32, z32, z16, z32, small, wa_pad, b_a, g_out, s0)
    return go, s_out


def _outproj_kernel(fo_ref, go_ref, wt_ref, wb_ref, x_ref, o_ref):
    o_ref[...] = (x_ref[...] + jnp.dot(fo_ref[...], wt_ref[...], preferred_element_type=F32)
                  + jnp.dot(go_ref[...], wb_ref[...], preferred_element_type=F32))


def _outproj(fo, go, w_out, x):
    t, d = x.shape
    half = fo.shape[1]
    tm = _pick(t, (512, 256, 128))
    tn = _pick(d, (1024, 512))
    return pl.pallas_call(
        _outproj_kernel,
        grid=(d // tn, t // tm),
        in_specs=[
            pl.BlockSpec((tm, half), lambda j, i: (i, 0)),
            pl.BlockSpec((tm, half), lambda j, i: (i, 0)),
            pl.BlockSpec((half, tn), lambda j, i: (0, j)),
            pl.BlockSpec((half, tn), lambda j, i: (1, j)),
            pl.BlockSpec((tm, tn), lambda j, i: (i, j)),
        ],
        out_specs=pl.BlockSpec((tm, tn), lambda j, i: (i, j)),
        out_shape=jax.ShapeDtypeStruct((t, d), F32),
        compiler_params=_params("parallel", "parallel"),
        name="outproj",
    )(fo, go, w_out, w_out, x)


def _mem_kernel(h_ref, gq_ref, wq_ref, mk_ref, mv_ref, wo_ref, gf_ref, h2_ref, n3t_ref, *, per_batch, n_mem):
    h = h_ref[...]
    tm = h.shape[0]
    n2 = (h * lax.rsqrt(jnp.mean(h * h, axis=-1, keepdims=True) + EPS) * gq_ref[...]).astype(BF16)
    q = jnp.dot(n2, wq_ref[...], preferred_element_type=F32)
    scale = MEM_HEAD_DIM ** -0.5
    rows_per = min(per_batch, tm)
    parts = []
    for b in range(tm // rows_per):
        heads = []
        for hd in range(MEM_HEADS):
            cs = slice(hd * MEM_HEAD_DIM, (hd + 1) * MEM_HEAD_DIM)
            qh = q[b * rows_per:(b + 1) * rows_per, cs].astype(BF16)
            kh = mk_ref[b * n_mem:(b + 1) * n_mem, cs].astype(BF16)
            vh = mv_ref[b * n_mem:(b + 1) * n_mem, cs].astype(BF16)
            s = lax.dot_general(qh, kh, (((1,), (1,)), ((), ())), preferred_element_type=F32) * scale
            s = s - jnp.max(s, axis=1, keepdims=True)
            p = jnp.exp(s)
            p = p / jnp.sum(p, axis=1, keepdims=True)
            heads.append(jnp.dot(p.astype(BF16), vh, preferred_element_type=F32))
        parts.append(jnp.concatenate(heads, axis=1))
    o = jnp.concatenate(parts, axis=0).astype(BF16)
    h2 = h + jnp.dot(o, wo_ref[...], preferred_element_type=F32)
    h2_ref[...] = h2
    n3 = h2 * lax.rsqrt(jnp.mean(h2 * h2, axis=-1, keepdims=True) + EPS) * gf_ref[...]
    n3t_ref[...] = n3.T.astype(BF16)


def _mem_block(h1, g_q, w_mq, mk, mv, w_mo, g_ffn, per_batch, n_mem):
    t, d = h1.shape
    tm = _pick(t, (256, 128)) if per_batch >= 256 else LANES
    nb = max(tm // per_batch, 1)
    tiles_per_batch = max(per_batch // tm, 1)
    return pl.pallas_call(
        functools.partial(_mem_kernel, per_batch=per_batch, n_mem=n_mem),
        grid=(t // tm,),
        in_specs=[
            pl.BlockSpec((tm, d), lambda i: (i, 0)),
            pl.BlockSpec((1, d), lambda i: (0, 0)),
            pl.BlockSpec((d, MEM_WIDTH), lambda i: (0, 0)),
            pl.BlockSpec((nb * n_mem, MEM_WIDTH), lambda i: (i // tiles_per_batch, 0)),
            pl.BlockSpec((nb * n_mem, MEM_WIDTH), lambda i: (i // tiles_per_batch, 0)),
            pl.BlockSpec((MEM_WIDTH, d), lambda i: (0, 0)),
            pl.BlockSpec((1, d), lambda i: (0, 0)),
        ],
        out_specs=[pl.BlockSpec((tm, d), lambda i: (i, 0)), pl.BlockSpec((d, tm), lambda i: (0, i))],
        out_shape=[jax.ShapeDtypeStruct((t, d), F32), jax.ShapeDtypeStruct((d, t), BF16)],
        compiler_params=_params("parallel"),
        name="mem_attn",
    )(h1, g_q, w_mq, mk, mv, w_mo, g_ffn)


def _extract_top(vals, key, count):
    cur = vals
    rank = jnp.full(vals.shape, float(count), F32)
    tops = []
    big = jnp.float32(2**30)
    for r in range(count):
        m = jnp.max(cur, axis=0, keepdims=True)
        first = jnp.min(jnp.where(cur == m, key, big), axis=0, keepdims=True)
        hit = key == first
        rank = jnp.where(hit, float(r), rank)
        cur = jnp.where(hit, -jnp.inf, cur)
        tops.append(m)
    return tops, rank


def _peer_route_kernel(w_ref, n3t_ref, keys_ref, pk2_ref, pk1_ref):
    K = PEER_TOPK
    qp = jnp.dot(w_ref[...], n3t_ref[...], preferred_element_type=F32)
    s1 = jnp.dot(keys_ref[0], qp[:PEER_HALF].astype(BF16), preferred_element_type=F32)
    s2 = jnp.dot(keys_ref[1], qp[PEER_HALF:].astype(BF16), preferred_element_type=F32)
    tm = s1.shape[1]
    kiota = lax.broadcasted_iota(jnp.int32, s1.shape, 0).astype(F32)
    a, rank1 = _extract_top(s1, kiota, K)
    b, rank2 = _extract_top(s2, kiota, K)

    bmat = jnp.concatenate(b, axis=0)
    jio = lax.broadcasted_iota(jnp.int32, (K, tm), 0).astype(F32)
    blocks, keys = [], []
    half = K // 2
    for i in range(half):
        blocks.append(jnp.where(jio < float(K // (i + 1)), a[i] + bmat, -jnp.inf))
        keys.append(jio + float(i * K))
    blocks.append(jnp.concatenate([a[i] + b[0] for i in range(half, K)], axis=0))
    keys.append((lax.broadcasted_iota(jnp.int32, (K - half, tm), 0).astype(F32) + float(half)) * float(K))
    cand = jnp.concatenate(blocks, axis=0)
    ckey = jnp.concatenate(keys, axis=0)
    _, crank = _extract_top(cand, ckey, K)
    sel = crank < float(K)
    z = jnp.sum(jnp.where(sel, jnp.exp(cand - (a[0] + b[0])), 0.0), axis=0, keepdims=True)
    self32 = sel.astype(F32)
    counts = [jnp.sum(self32[i * K:(i + 1) * K], axis=0, keepdims=True) for i in range(half)]
    counts += [self32[half * K + i:half * K + i + 1] for i in range(K - half)]

    cnt = jnp.zeros(s1.shape, F32)
    for i in range(K):
        cnt = jnp.where(rank1 == float(i), counts[i], cnt)
    pk2_ref[0, 0] = rank2.astype(BF16)
    pk2_ref[0, 1] = jnp.exp(s2 - b[0]).astype(BF16)
    pk1_ref[0, 0] = cnt
    pk1_ref[0, 1] = jnp.exp(s1 - a[0]) / z


def _peer_route(w_pq_t, n3t, keys):
    d, t = n3t.shape
    tm = _pick(t, (512, 256, 128))
    qd = 2 * PEER_HALF
    spec = pl.BlockSpec((1, 2, PEER_NKEYS, tm), lambda i, h: (h, 0, 0, i))
    return pl.pallas_call(
        _peer_route_kernel,
        grid=(t // tm, PEER_HEADS),
        in_specs=[
            pl.BlockSpec((qd, d), lambda i, h: (h, 0)),
            pl.BlockSpec((d, tm), lambda i, h: (0, i)),
            pl.BlockSpec((2, PEER_NKEYS, PEER_HALF), lambda i, h: (0, 0, 0)),
        ],
        out_specs=[spec, spec],
        out_shape=[jax.ShapeDtypeStruct((PEER_HEADS, 2, PEER_NKEYS, t), BF16),
                   jax.ShapeDtypeStruct((PEER_HEADS, 2, PEER_NKEYS, t), F32)],
        compiler_params=_params("parallel", "parallel"),
        name="peer_route",
    )(w_pq_t, n3t, keys)


def _gelu_tanh(x):
    inner = x * (0.7978845608028654 + (0.7978845608028654 * 0.044715) * (x * x))
    half = 0.5 * x
    return half + half * jnp.tanh(inner)


def _peer_kernel(n3t_ref, pk2_ref, pk1_ref, u_ref, vt_ref, o_ref, act_sc, *, ec, n_chunks):
    e = pl.program_id(1)
    cur = e % 2

    @pl.when(e == 0)
    def _():
        o_ref[...] = jnp.zeros_like(o_ref)
        act_sc[1] = jnp.zeros(act_sc.shape[1:], act_sc.dtype)

    h = jnp.dot(u_ref[...], n3t_ref[...], preferred_element_type=F32)
    o_ref[...] += jnp.dot(vt_ref[...], act_sc[1 - cur], preferred_element_type=F32)
    k1_base = jnp.minimum(e, n_chunks - 1) * (ec // PEER_NKEYS)
    for kk in range(ec // PEER_NKEYS):
        rows = slice(kk * PEER_NKEYS, (kk + 1) * PEER_NKEYS)
        gate = jnp.zeros((PEER_NKEYS, h.shape[1]), BF16)
        for hd in range(PEER_HEADS):
            cnt = pk1_ref[hd, 0, pl.ds(k1_base + kk, 1), :].astype(BF16)
            e1 = pk1_ref[hd, 1, pl.ds(k1_base + kk, 1), :].astype(BF16)
            gate = gate + jnp.where(pk2_ref[hd, 0] < cnt, pk2_ref[hd, 1], 0.0) * e1
        act_sc[cur, rows, :] = (_gelu_tanh(h[rows]) * gate.astype(F32)).astype(BF16)


def _peer(n3t, pk2, pk1, u, vt):
    d, t = n3t.shape
    n_exp = u.shape[0]
    tm = _pick(t, (512, 256, 128))
    ec = PEER_EXPERT_CHUNK
    n_chunks = n_exp // ec
    once = pl.Buffered(1)
    return pl.pallas_call(
        functools.partial(_peer_kernel, ec=ec, n_chunks=n_chunks),
        grid=(t // tm, n_chunks + 1),
        in_specs=[
            pl.BlockSpec((d, tm), lambda i, e: (0, i), pipeline_mode=once),
            pl.BlockSpec((PEER_HEADS, 2, PEER_NKEYS, tm), lambda i, e: (0, 0, 0, i), pipeline_mode=once),
            pl.BlockSpec((PEER_HEADS, 2, PEER_NKEYS, tm), lambda i, e: (0, 0, 0, i), pipeline_mode=once),
            pl.BlockSpec((ec, d), lambda i, e: (jnp.minimum(e, n_chunks - 1), 0)),
            pl.BlockSpec((d, ec), lambda i, e: (0, jnp.maximum(e - 1, 0))),
        ],
        out_specs=pl.BlockSpec((d, tm), lambda i, e: (0, i)),
        out_shape=jax.ShapeDtypeStruct((d, t), F32),
        scratch_shapes=[pltpu.VMEM((2, ec, tm), BF16)],
        compiler_params=_params("parallel", "arbitrary"),
        name="peer_experts",
    )(n3t, pk2, pk1, u, vt)


def _final_kernel(h_ref, pt_ref, g_ref, o_ref):
    x = h_ref[...] + pt_ref[...].T
    y = x * lax.rsqrt(jnp.mean(x * x, axis=-1, keepdims=True) + EPS)
    o_ref[...] = y * g_ref[...]


def _final(h2, peer_t, g):
    t, d = h2.shape
    tm = _pick(t, (256, 128))
    return pl.pallas_call(
        _final_kernel,
        grid=(t // tm,),
        in_specs=[pl.BlockSpec((tm, d), lambda i: (i, 0)), pl.BlockSpec((d, tm), lambda i: (0, i)),
                  pl.BlockSpec((1, d), lambda i: (0, 0))],
        out_specs=pl.BlockSpec((tm, d), lambda i: (i, 0)),
        out_shape=jax.ShapeDtypeStruct((t, d), F32),
        compiler_params=_params("parallel"),
        name="final_norm",
    )(h2, peer_t, g)


def _layer(x, w, mem_k, mem_v, batch, tokens, n_mem, past):
    n = _rmsnorm(x, w["norm_mix"])
    z32, z16 = _mm(n, w["w_main"], (F32, BF16))
    small = _small_proj(n, w["w_small"], w["b_fgate"])
    fk = z32[:, COL_FK:COL_FK + FOX_WIDTH]
    fv = z32[:, COL_FV:COL_FV + FOX_WIDTH]
    logf = small[:, :FOX_HEADS]

    if past is None:
        lb = _pick(tokens, (512, 256, 128))
        c = _cumsum_t(small.reshape(batch, tokens, LANES), lb)
        tq = _pick(tokens, (512, 256, 128))
        c4 = c.reshape(batch * LANES, tokens // tq, 1, tq)
        fo = _fox_prompt(z16, c4, batch, tokens)
        go, gla_state = _gla(z32, z16, small, w["wa_pad"], w["b_gla_a"], w["norm_gla_out"], None, batch, tokens)
    else:
        ck, cv, clogf, s0 = past
        plen = ck.shape[1]
        lf = jnp.concatenate([
            jnp.pad(clogf.astype(F32), ((0, 0), (0, 0), (0, LANES - FOX_HEADS))),
            small.reshape(batch, tokens, LANES),
            jnp.zeros((batch, LANES - tokens, LANES), F32)], axis=1)
        c = _cumsum_t(lf, LANES)
        c3 = c.reshape(batch * LANES, 1, plen + LANES)
        fo = _fox_sample(z16, ck.reshape(batch * plen, FOX_WIDTH), cv.reshape(batch * plen, FOX_WIDTH), c3,
                         batch, tokens, plen)
        go, gla_state = _gla(z32, z16, small, w["wa_pad"], w["b_gla_a"], w["norm_gla_out"], s0.astype(F32),
                             batch, tokens)

    h1 = _outproj(fo, go, w["w_out"], x)
    h2, n3t = _mem_block(h1, w["norm_mem_q"], w["w_mq"], mem_k, mem_v, w["w_mo"], w["norm_ffn"], tokens, n_mem)
    pk2, pk1 = _peer_route(w["w_pq_t"], n3t, w["sub_keys"])
    peer_t = _peer(n3t, pk2, pk1, w["expert_u"], w["expert_v_t"])
    return h2, peer_t, fk, fv, logf, gla_state


def kernel(x_prompt, x_sample, cache_fox_k, cache_fox_v, cache_fox_logf, state_gla, cache_mem_k, cache_mem_v, mem_prompt, norm_mix, w_in, b_fgate, w_gla_a2, b_gla_a, norm_gla_out, w_out, norm_mem_q, norm_mem_kv, w_mq, w_mk, w_mv, w_mo, norm_ffn, w_pq, sub_keys, expert_u, expert_v, norm_final):
    depth = w_in.shape[0]
    assert depth == 1, "one trunk layer"
    bp, sp, d = x_prompt.shape
    bs, ss, _ = x_sample.shape
    n_mem = mem_prompt.shape[1]
    l = 0

    wi = w_in[l]
    o_ff = 3 * FOX_WIDTH
    o_g = o_ff + FOX_HEADS
    o_ga = o_g + 2 * GLA_KEY_WIDTH + 2 * GLA_WIDTH
    w = {
        "norm_mix": norm_mix[l],
        "w_main": jnp.concatenate([wi[:, :o_ff], wi[:, o_g:o_ga]], axis=1).astype(BF16),
        "w_small": jnp.concatenate([wi[:, o_ff:o_g], wi[:, o_ga:],
                                    jnp.zeros((d, LANES - FOX_HEADS - GLA_GATE_RANK), F32)], axis=1).astype(BF16),
        "b_fgate": jnp.pad(b_fgate[l], (0, LANES - FOX_HEADS)).reshape(1, LANES),
        "wa_pad": jnp.pad(w_gla_a2[l], ((FOX_HEADS, LANES - FOX_HEADS - GLA_GATE_RANK), (0, 0))).astype(BF16),
        "b_gla_a": b_gla_a[l].reshape(1, GLA_KEY_WIDTH),
        "norm_gla_out": norm_gla_out[l].reshape(1, GLA_DV),
        "w_out": w_out[l].astype(BF16),
        "norm_mem_q": norm_mem_q[l].reshape(1, d),
        "w_mq": w_mq[l].astype(BF16),
        "w_mo": w_mo[l].astype(BF16),
        "norm_ffn": norm_ffn[l].reshape(1, d),
        "w_pq_t": w_pq[l].T.astype(BF16),
        "sub_keys": sub_keys[l].astype(BF16),
        "expert_u": expert_u[l].astype(BF16),
        "expert_v_t": expert_v[l].T.astype(BF16),
    }

    m = _rmsnorm(mem_prompt.reshape(bp * n_mem, d), norm_mem_kv[l])
    (mk,) = _mm(m, w_mk[l].astype(BF16), (F32,))
    (mv,) = _mm(m, w_mv[l].astype(BF16), (F32,))

    hp, pp, fkp, fvp, lfp, gsp = _layer(x_prompt.reshape(bp * sp, d), w, mk, mv, bp, sp, n_mem, None)
    past = (cache_fox_k[l], cache_fox_v[l], cache_fox_logf[l], state_gla[l])
    hs, ps, fks, fvs, lfs, gss = _layer(x_sample.reshape(bs * ss, d), w,
                                        cache_mem_k[l].reshape(bs * n_mem, MEM_WIDTH),
                                        cache_mem_v[l].reshape(bs * n_mem, MEM_WIDTH), bs, ss, n_mem, past)

    g_fin = norm_final.reshape(1, d)
    y_prompt = _final(hp, pp, g_fin).reshape(bp, sp, d)
    y_sample = _final(hs, ps, g_fin).reshape(bs, ss, d)
    hshape_p = (1, bp, sp, FOX_HEADS, FOX_HEAD_DIM)
    hshape_s = (1, bs, ss, FOX_HEADS, FOX_HEAD_DIM)
    return (y_prompt, y_sample,
            fkp.reshape(hshape_p), fvp.reshape(hshape_p), lfp.reshape(1, bp, sp, FOX_HEADS), gsp[None],
            mk.reshape(1, bp, n_mem, MEM_HEADS, MEM_HEAD_DIM), mv.reshape(1, bp, n_mem, MEM_HEADS, MEM_HEAD_DIM),
            fks.reshape(hshape_s), fvs.reshape(hshape_s), lfs.reshape(1, bs, ss, FOX_HEADS), gss[None])
```

```python
import functools

import jax
import jax.numpy as jnp
from jax import lax
from jax.experimental import pallas as pl
from jax.experimental.pallas import tpu as pltpu

F32 = jnp.float32
BF16 = jnp.bfloat16

EPS = 1e-6
NEG_INF = -1e30

FOX_HEADS = 16
FOX_HEAD_DIM = 128
FOX_WIDTH = FOX_HEADS * FOX_HEAD_DIM
FOX_HEADS_PER_STEP = 4
LOG2E = 1.4426950408889634
GLA_HEADS = 4
GLA_DK = 256
GLA_DV = 512
GLA_KEY_WIDTH = GLA_HEADS * GLA_DK
GLA_WIDTH = GLA_HEADS * GLA_DV
GLA_GATE_RANK = 16
GLA_TAU = 16.0
GLA_CHUNK = 128
GLA_SUB = 32
MEM_HEADS = 4
MEM_HEAD_DIM = 128
MEM_WIDTH = MEM_HEADS * MEM_HEAD_DIM
PEER_HEADS = 8
PEER_NKEYS = 128
PEER_HALF = 128
PEER_TOPK = 16
PEER_EXPERT_CHUNK = 512
LANES = 128
MAIN_WIDTH = 3 * FOX_WIDTH + 2 * GLA_KEY_WIDTH + 2 * GLA_WIDTH

COL_FQ, COL_FK, COL_FV = 0, FOX_WIDTH, 2 * FOX_WIDTH
COL_GQ = 3 * FOX_WIDTH
COL_GK = COL_GQ + GLA_KEY_WIDTH
COL_GV = COL_GK + GLA_KEY_WIDTH
COL_GG = COL_GV + GLA_WIDTH

VMEM_LIMIT_BYTES = 52 * 2**20


def _pick(n, prefs):
    for p in prefs:
        if n % p == 0:
            return p
    raise ValueError(f"no tile in {prefs} divides {n}")


def _params(*sem):
    return pltpu.CompilerParams(dimension_semantics=sem, vmem_limit_bytes=VMEM_LIMIT_BYTES)


def _log_sigmoid(x):
    return -(jnp.maximum(-x, 0.0) + jnp.log1p(jnp.exp(-jnp.abs(x))))


def _split3(x):
    hi = x.astype(BF16)
    r1 = x - hi.astype(F32)
    mid = r1.astype(BF16)
    lo = (r1 - mid.astype(F32)).astype(BF16)
    return hi, mid, lo


def _rmsnorm_kernel(x_ref, g_ref, o_ref):
    x = x_ref[...]
    y = x * lax.rsqrt(jnp.mean(x * x, axis=-1, keepdims=True) + EPS)
    o_ref[...] = (y * g_ref[...]).astype(o_ref.dtype)


def _rmsnorm(x, g):
    t, d = x.shape
    tm = _pick(t, (512, 256, 128))
    return pl.pallas_call(
        _rmsnorm_kernel,
        grid=(t // tm,),
        in_specs=[pl.BlockSpec((tm, d), lambda i: (i, 0)), pl.BlockSpec((1, d), lambda i: (0, 0))],
        out_specs=pl.BlockSpec((tm, d), lambda i: (i, 0)),
        out_shape=jax.ShapeDtypeStruct((t, d), BF16),
        compiler_params=_params("parallel"),
        name="rmsnorm",
    )(x, g.reshape(1, d))


def _mm_kernel(a_ref, w_ref, *o_refs):
    r = jnp.dot(a_ref[...], w_ref[...], preferred_element_type=F32)
    for o_ref in o_refs:
        o_ref[...] = r.astype(o_ref.dtype)


def _mm(a, w, out_dtypes):
    t, k = a.shape
    n = w.shape[1]
    tm = _pick(t, (512, 256, 128))
    tn = _pick(n, (1024, 512, 256, 128))
    outs = pl.pallas_call(
        _mm_kernel,
        grid=(n // tn, t // tm),
        in_specs=[pl.BlockSpec((tm, k), lambda j, i: (i, 0)), pl.BlockSpec((k, tn), lambda j, i: (0, j))],
        out_specs=[pl.BlockSpec((tm, tn), lambda j, i: (i, j)) for _ in out_dtypes],
        out_shape=[jax.ShapeDtypeStruct((t, n), dt) for dt in out_dtypes],
        compiler_params=_params("parallel", "parallel"),
        name="proj",
    )(a, w)
    return outs


def _small_kernel(n_ref, w_ref, b_ref, o_ref):
    z = jnp.dot(n_ref[...], w_ref[...], preferred_element_type=F32)
    col = lax.broadcasted_iota(jnp.int32, z.shape, 1)
    o_ref[...] = jnp.where(col < FOX_HEADS, _log_sigmoid(z + b_ref[...]), z)


def _small_proj(n, w_small, b_fgate_pad):
    t, k = n.shape
    tm = _pick(t, (512, 256, 128))
    return pl.pallas_call(
        _small_kernel,
        grid=(t // tm,),
        in_specs=[pl.BlockSpec((tm, k), lambda i: (i, 0)), pl.BlockSpec((k, LANES), lambda i: (0, 0)),
                  pl.BlockSpec((1, LANES), lambda i: (0, 0))],
        out_specs=pl.BlockSpec((tm, LANES), lambda i: (i, 0)),
        out_shape=jax.ShapeDtypeStruct((t, LANES), F32),
        compiler_params=_params("parallel"),
        name="small_proj",
    )(n, w_small, b_fgate_pad)


def _cumsum_kernel(x_ref, o_ref, carry_ref, *, lb):
    @pl.when(pl.program_id(1) == 0)
    def _():
        carry_ref[...] = jnp.zeros_like(carry_ref)

    xt = x_ref[0].T
    r = lax.broadcasted_iota(jnp.int32, (lb, lb), 0)
    c = lax.broadcasted_iota(jnp.int32, (lb, lb), 1)
    tri = (r <= c).astype(BF16)
    acc = carry_ref[...]
    for part in _split3(xt):
        acc = acc + jnp.dot(part, tri, preferred_element_type=F32)
    o_ref[0] = acc
    carry_ref[...] = jnp.broadcast_to(acc[:, lb - 1:lb], carry_ref.shape)


def _cumsum_t(x, lb):
    b, l, _ = x.shape
    return pl.pallas_call(
        functools.partial(_cumsum_kernel, lb=lb),
        grid=(b, l // lb),
        in_specs=[pl.BlockSpec((1, lb, LANES), lambda i, j: (i, j, 0))],
        out_specs=pl.BlockSpec((1, LANES, lb), lambda i, j: (i, 0, j)),
        out_shape=jax.ShapeDtypeStruct((b, LANES, l), F32),
        scratch_shapes=[pltpu.VMEM((LANES, lb), F32)],
        compiler_params=_params("parallel", "arbitrary"),
        name="cumsum_logf",
    )(x)


def _fox_prompt_kernel(q_ref, k_ref, v_ref, c_ref, o_ref, m_sc, acc_sc, *, tq, hb):
    qi = pl.program_id(2)
    qscale = FOX_HEAD_DIM ** -0.5 * LOG2E
    m_sc[...] = jnp.full_like(m_sc, NEG_INF)
    acc_sc[...] = jnp.zeros_like(acc_sc)

    def chunk(kc, masked):
        ks = pl.multiple_of(kc * tq, tq)
        ones = jnp.ones((tq, LANES), BF16)
        for hh in range(hb):
            cs = slice(hh * FOX_HEAD_DIM, (hh + 1) * FOX_HEAD_DIM)
            s = lax.dot_general(q_ref[:, cs], k_ref[pl.ds(ks, tq), cs], (((1,), (1,)), ((), ())),
                                preferred_element_type=F32) * qscale - c_ref[hh, kc] * LOG2E
            if masked:
                r = lax.broadcasted_iota(jnp.int32, s.shape, 0)
                c = lax.broadcasted_iota(jnp.int32, s.shape, 1)
                s = jnp.where(r >= c, s, NEG_INF)
            m_prev = m_sc[hh]
            m_new = jnp.maximum(m_prev, jnp.max(s, axis=1, keepdims=True))
            alpha = jnp.exp2(m_prev - m_new)
            p = jnp.exp2(s - jnp.concatenate([m_new] * (tq // LANES), axis=1))
            v_ext = jnp.concatenate([v_ref[pl.ds(ks, tq), cs], ones], axis=1)
            pv = jnp.dot(p.astype(BF16), v_ext, preferred_element_type=F32)
            acc_sc[hh] = jnp.concatenate([alpha, alpha], axis=1) * acc_sc[hh] + pv
            m_sc[hh] = m_new

    def body(kc, carry):
        chunk(kc, False)
        return carry

    lax.fori_loop(0, qi, body, 0)
    chunk(qi, True)
    for hh in range(hb):
        acc = acc_sc[hh]
        o_ref[:, hh * FOX_HEAD_DIM:(hh + 1) * FOX_HEAD_DIM] = (
            acc[:, :FOX_HEAD_DIM] / acc[:, FOX_HEAD_DIM:]).astype(o_ref.dtype)


def _fox_prompt(z16, c4, batch, seq):
    tq = _pick(seq, (512, 256, 128))
    nq = seq // tq
    hb = FOX_HEADS_PER_STEP
    wid = hb * FOX_HEAD_DIM
    kb, vb = COL_FK // wid, COL_FV // wid
    return pl.pallas_call(
        functools.partial(_fox_prompt_kernel, tq=tq, hb=hb),
        grid=(batch, FOX_HEADS // hb, nq),
        in_specs=[
            pl.BlockSpec((tq, wid), lambda b, h, i: (b * nq + i, h)),
            pl.BlockSpec((seq, wid), lambda b, h, i: (b, kb + h)),
            pl.BlockSpec((seq, wid), lambda b, h, i: (b, vb + h)),
            pl.BlockSpec((hb, nq, 1, tq), lambda b, h, i: (b * (LANES // hb) + h, 0, 0, 0)),
        ],
        out_specs=pl.BlockSpec((tq, wid), lambda b, h, i: (b * nq + i, h)),
        out_shape=jax.ShapeDtypeStruct((batch * seq, FOX_WIDTH), BF16),
        scratch_shapes=[pltpu.VMEM((hb, tq, LANES), F32), pltpu.VMEM((hb, tq, 2 * FOX_HEAD_DIM), F32)],
        compiler_params=_params("parallel", "parallel", "arbitrary"),
        name="fox_prompt",
    )(z16, z16, z16, c4)


def _fox_sample_kernel(q_ref, kn_ref, vn_ref, ck_ref, cv_ref, c_ref, o_ref, *, past, new):
    scale = FOX_HEAD_DIM ** -0.5
    q = q_ref[...]
    pad = jnp.zeros((LANES - new, FOX_HEAD_DIM), BF16)
    kn = jnp.concatenate([kn_ref[...], pad], axis=0)
    vn = jnp.concatenate([vn_ref[...], pad], axis=0)
    dn = (((1,), (1,)), ((), ()))
    c = c_ref[0]
    s1 = lax.dot_general(q, ck_ref[...].astype(BF16), dn, preferred_element_type=F32) * scale - c[:, :past]
    s2 = lax.dot_general(q, kn, dn, preferred_element_type=F32) * scale - c[:, past:]
    r = lax.broadcasted_iota(jnp.int32, s2.shape, 0)
    cc = lax.broadcasted_iota(jnp.int32, s2.shape, 1)
    s2 = jnp.where(r >= cc, s2, NEG_INF)
    m = jnp.maximum(jnp.max(s1, axis=1, keepdims=True), jnp.max(s2, axis=1, keepdims=True))
    p1 = jnp.exp(s1 - m)
    p2 = jnp.exp(s2 - m)
    l = jnp.sum(p1, axis=1, keepdims=True) + jnp.sum(p2, axis=1, keepdims=True)
    o = (jnp.dot(p1.astype(BF16), cv_ref[...].astype(BF16), preferred_element_type=F32)
         + jnp.dot(p2.astype(BF16), vn, preferred_element_type=F32))
    o_ref[...] = (o / l).astype(o_ref.dtype)


def _fox_sample(z16, ck, cv, c3, batch, new, past):
    kb, vb = COL_FK // FOX_HEAD_DIM, COL_FV // FOX_HEAD_DIM
    return pl.pallas_call(
        functools.partial(_fox_sample_kernel, past=past, new=new),
        grid=(batch, FOX_HEADS),
        in_specs=[
            pl.BlockSpec((new, FOX_HEAD_DIM), lambda b, h: (b, h)),
            pl.BlockSpec((new, FOX_HEAD_DIM), lambda b, h: (b, kb + h)),
            pl.BlockSpec((new, FOX_HEAD_DIM), lambda b, h: (b, vb + h)),
            pl.BlockSpec((past, FOX_HEAD_DIM), lambda b, h: (b, h)),
            pl.BlockSpec((past, FOX_HEAD_DIM), lambda b, h: (b, h)),
            pl.BlockSpec((1, 1, past + LANES), lambda b, h: (b * LANES + h, 0, 0)),
        ],
        out_specs=pl.BlockSpec((new, FOX_HEAD_DIM), lambda b, h: (b, h)),
        out_shape=jax.ShapeDtypeStruct((batch * new, FOX_WIDTH), BF16),
        compiler_params=_params("parallel", "parallel"),
        name="fox_sample",
    )(z16, z16, z16, ck, cv, c3)


def _gla_kernel(q_ref, k_ref, v_ref, gg_ref, sm_ref, wa_ref, ba_ref, g_ref, s0_ref, go_ref, sout_ref, s_sc,
                *, valid, has_init):
    ci = pl.program_id(2)
    C = GLA_CHUNK

    @pl.when(ci == 0)
    def _():
        if has_init:
            s_sc[...] = s0_ref[0, 0]
        else:
            s_sc[...] = jnp.zeros_like(s_sc)

    def rows(ref, dtype):
        x = ref[...].astype(dtype)
        if valid < C:
            x = jnp.concatenate([x, jnp.zeros((C - valid, x.shape[1]), dtype)], axis=0)
        return x

    row = lax.broadcasted_iota(jnp.int32, (C, C), 0)
    col = lax.broadcasted_iota(jnp.int32, (C, C), 1)

    a = jnp.dot(rows(sm_ref, BF16), wa_ref[...], preferred_element_type=F32) + ba_ref[...]
    lg = _log_sigmoid(a) * (1.0 / GLA_TAU)
    if valid < C:
        lg = jnp.where(lax.broadcasted_iota(jnp.int32, lg.shape, 0) < valid, lg, 0.0)
    low = (row >= col).astype(BF16)
    bc = jnp.zeros_like(lg)
    for part in _split3(lg):
        bc = bc + jnp.dot(low, part, preferred_element_type=F32)

    q = rows(q_ref, F32) * (GLA_DK ** -0.5)
    k = rows(k_ref, F32)
    v = rows(v_ref, BF16)
    s = s_sc[...]

    o_state = jnp.dot((q * jnp.exp(bc)).astype(BF16), s.astype(BF16), preferred_element_type=F32)
    krow = lax.broadcasted_iota(jnp.int32, (C, GLA_DK), 0)
    sub_row = lax.broadcasted_iota(jnp.int32, (GLA_SUB, C), 0)
    sub_col = lax.broadcasted_iota(jnp.int32, (GLA_SUB, C), 1)
    outs = []
    for i in range(C // GLA_SUB):
        r0, r1 = i * GLA_SUB, (i + 1) * GLA_SUB
        if valid <= r0:
            outs.append(o_state[r0:r1])
            continue
        base = bc[r0 - 1:r0] if i > 0 else jnp.zeros((1, GLA_DK), F32)
        qt = (q[r0:r1] * jnp.exp(bc[r0:r1] - base)).astype(BF16)
        kt = jnp.where(krow < r1, k * jnp.exp(base - bc), 0.0).astype(BF16)
        att = lax.dot_general(qt, kt, (((1,), (1,)), ((), ())), preferred_element_type=F32)
        att = jnp.where(sub_col <= sub_row + r0, att, 0.0)
        outs.append(o_state[r0:r1] + jnp.dot(att.astype(BF16), v, preferred_element_type=F32))
    o = jnp.concatenate(outs, axis=0)

    b_end = bc[C - 1:C]
    kd = k * jnp.exp(b_end - bc)
    decay = jnp.broadcast_to(jnp.exp(b_end), (LANES, GLA_DK)).T
    decay = jnp.concatenate([decay] * (GLA_DV // LANES), axis=1)
    s_new = decay * s + jnp.dot(kd.T.astype(BF16), v, preferred_element_type=F32)
    s_sc[...] = s_new

    @pl.when(ci == pl.num_programs(2) - 1)
    def _():
        sout_ref[0, 0] = s_new

    o = o[:valid]
    on = o * lax.rsqrt(jnp.mean(o * o, axis=-1, keepdims=True) + EPS) * g_ref[...]
    gg = gg_ref[...]
    go_ref[...] = (on * (gg * jax.nn.sigmoid(gg))).astype(go_ref.dtype)


def _gla(z32, z16, small, wa_pad, b_a, g_out, s0, batch, tokens):
    if tokens >= GLA_CHUNK:
        assert tokens % GLA_CHUNK == 0
        blk, nch = GLA_CHUNK, tokens // GLA_CHUNK
    else:
        blk, nch = tokens, 1
    has_init = s0 is not None
    if s0 is None:
        s0 = jnp.zeros((1, 1, GLA_DK, GLA_DV), F32)
        s0_map = lambda b, h, c: (0, 0, 0, 0)
    else:
        s0_map = lambda b, h, c: (b, h, 0, 0)
    qb, kb = COL_GQ // GLA_DK, COL_GK // GLA_DK
    vb, gb = COL_GV // GLA_DV, COL_GG // GLA_DV
    go, s_out = pl.pallas_call(
        functools.partial(_gla_kernel, valid=blk, has_init=has_init),
        grid=(batch, GLA_HEADS, nch),
        in_specs=[
            pl.BlockSpec((blk, GLA_DK), lambda b, h, c: (b * nch + c, qb + h)),
            pl.BlockSpec((blk, GLA_DK), lambda b, h, c: (b * nch + c, kb + h)),
            pl.BlockSpec((blk, GLA_DV), lambda b, h, c: (b * nch + c, vb + h)),
            pl.BlockSpec((blk, GLA_DV), lambda b, h, c: (b * nch + c, gb + h)),
            pl.BlockSpec((blk, LANES), lambda b, h, c: (b * nch + c, 0)),
            pl.BlockSpec((LANES, GLA_DK), lambda b, h, c: (0, h)),
            pl.BlockSpec((1, GLA_DK), lambda b, h, c: (0, h)),
            pl.BlockSpec((1, GLA_DV), lambda b, h, c: (0, 0)),
            pl.BlockSpec((1, 1, GLA_DK, GLA_DV), s0_map),
        ],
        out_specs=[
            pl.BlockSpec((blk, GLA_DV), lambda b, h, c: (b * nch + c, h)),
            pl.BlockSpec((1, 1, GLA_DK, GLA_DV), lambda b, h, c: (b, h, 0, 0)),
        ],
        out_shape=[jax.ShapeDtypeStruct((batch * tokens, GLA_WIDTH), BF16),
                   jax.ShapeDtypeStruct((batch, GLA_HEADS, GLA_DK, GLA_DV), F32)],
        scratch_shapes=[pltpu.VMEM((GLA_DK, GLA_DV), F32)],
        compiler_params=_params("parallel", "parallel", "arbitrary"),
        name="gla",
    )(z32, z32, z16, z32, small, wa_pad, b_a, g_out, s0)
    return go, s_out


def _outproj_kernel(fo_ref, go_ref, wt_ref, wb_ref, x_ref, o_ref):
    o_ref[...] = (x_ref[...] + jnp.dot(fo_ref[...], wt_ref[...], preferred_element_type=F32)
                  + jnp.dot(go_ref[...], wb_ref[...], preferred_element_type=F32))


def _outproj(fo, go, w_out, x):
    t, d = x.shape
    half = fo.shape[1]
    tm = _pick(t, (512, 256, 128))
    tn = _pick(d, (1024, 512))
    return pl.pallas_call(
        _outproj_kernel,
        grid=(d // tn, t // tm),
        in_specs=[
            pl.BlockSpec((tm, half), lambda j, i: (i, 0)),
            pl.BlockSpec((tm, half), lambda j, i: (i, 0)),
            pl.BlockSpec((half, tn), lambda j, i: (0, j)),
            pl.BlockSpec((half, tn), lambda j, i: (1, j)),
            pl.BlockSpec((tm, tn), lambda j, i: (i, j)),
        ],
        out_specs=pl.BlockSpec((tm, tn), lambda j, i: (i, j)),
        out_shape=jax.ShapeDtypeStruct((t, d), F32),
        compiler_params=_params("parallel", "parallel"),
        name="outproj",
    )(fo, go, w_out, w_out, x)


def _mem_kernel(h_ref, gq_ref, wq_ref, mk_ref, mv_ref, wo_ref, gf_ref, h2_ref, n3t_ref, *, per_batch, n_mem):
    h = h_ref[...]
    tm = h.shape[0]
    n2 = (h * lax.rsqrt(jnp.mean(h * h, axis=-1, keepdims=True) + EPS) * gq_ref[...]).astype(BF16)
    q = jnp.dot(n2, wq_ref[...], preferred_element_type=F32)
    scale = MEM_HEAD_DIM ** -0.5
    rows_per = min(per_batch, tm)
    parts = []
    for b in range(tm // rows_per):
        heads = []
        for hd in range(MEM_HEADS):
            cs = slice(hd * MEM_HEAD_DIM, (hd + 1) * MEM_HEAD_DIM)
            qh = q[b * rows_per:(b + 1) * rows_per, cs].astype(BF16)
            kh = mk_ref[b * n_mem:(b + 1) * n_mem, cs].astype(BF16)
            vh = mv_ref[b * n_mem:(b + 1) * n_mem, cs].astype(BF16)
            s = lax.dot_general(qh, kh, (((1,), (1,)), ((), ())), preferred_element_type=F32) * scale
            s = s - jnp.max(s, axis=1, keepdims=True)
            p = jnp.exp(s)
            p = p / jnp.sum(p, axis=1, keepdims=True)
            heads.append(jnp.dot(p.astype(BF16), vh, preferred_element_type=F32))
        parts.append(jnp.concatenate(heads, axis=1))
    o = jnp.concatenate(parts, axis=0).astype(BF16)
    h2 = h + jnp.dot(o, wo_ref[...], preferred_element_type=F32)
    h2_ref[...] = h2
    n3 = h2 * lax.rsqrt(jnp.mean(h2 * h2, axis=-1, keepdims=True) + EPS) * gf_ref[...]
    n3t_ref[...] = n3.T.astype(BF16)


def _mem_block(h1, g_q, w_mq, mk, mv, w_mo, g_ffn, per_batch, n_mem):
    t, d = h1.shape
    tm = _pick(t, (256, 128)) if per_batch >= 256 else LANES
    nb = max(tm // per_batch, 1)
    tiles_per_batch = max(per_batch // tm, 1)
    return pl.pallas_call(
        functools.partial(_mem_kernel, per_batch=per_batch, n_mem=n_mem),
        grid=(t // tm,),
        in_specs=[
            pl.BlockSpec((tm, d), lambda i: (i, 0)),
            pl.BlockSpec((1, d), lambda i: (0, 0)),
            pl.BlockSpec((d, MEM_WIDTH), lambda i: (0, 0)),
            pl.BlockSpec((nb * n_mem, MEM_WIDTH), lambda i: (i // tiles_per_batch, 0)),
            pl.BlockSpec((nb * n_mem, MEM_WIDTH), lambda i: (i // tiles_per_batch, 0)),
            pl.BlockSpec((MEM_WIDTH, d), lambda i: (0, 0)),
            pl.BlockSpec((1, d), lambda i: (0, 0)),
        ],
        out_specs=[pl.BlockSpec((tm, d), lambda i: (i, 0)), pl.BlockSpec((d, tm), lambda i: (0, i))],
        out_shape=[jax.ShapeDtypeStruct((t, d), F32), jax.ShapeDtypeStruct((d, t), BF16)],
        compiler_params=_params("parallel"),
        name="mem_attn",
    )(h1, g_q, w_mq, mk, mv, w_mo, g_ffn)


def _extract_top(vals, key, count, exact):
    cur = vals
    rank = jnp.full(vals.shape, float(count), F32)
    tops = []
    big = jnp.float32(2**30)
    for r in range(count):
        m = jnp.max(cur, axis=0, keepdims=True)
        hit = cur == m
        if exact:
            first = jnp.min(jnp.where(hit, key, big), axis=0, keepdims=True)
            hit = key == first
        rank = jnp.where(hit, float(r), rank)
        cur = jnp.where(hit, -jnp.inf, cur)
        tops.append(m)
    return tops, rank


def _peer_route_kernel(w_ref, n3t_ref, keys_ref, pk2_ref, pk1_ref, qp_sc):
    qp_sc[...] = jnp.dot(w_ref[...], n3t_ref[...], preferred_element_type=F32)
    qd = 2 * PEER_HALF

    def head(hd, carry):
        r0 = pl.multiple_of(hd * qd, qd)
        s1 = jnp.dot(keys_ref[0], qp_sc[pl.ds(r0, PEER_HALF), :].astype(BF16), preferred_element_type=F32)
        s2 = jnp.dot(keys_ref[1], qp_sc[pl.ds(r0 + PEER_HALF, PEER_HALF), :].astype(BF16),
                     preferred_element_type=F32)

        def store(tables):
            rank2, e2, cnt, e1 = tables
            pk2_ref[hd, 0] = rank2.astype(BF16)
            pk2_ref[hd, 1] = e2.astype(BF16)
            pk1_ref[hd, 0] = cnt
            pk1_ref[hd, 1] = e1

        tables, removed = _route_tables(s1, s2, exact=False)
        store(tables)
        ties = jnp.max(jnp.abs(removed - float(3 * PEER_TOPK)))

        @pl.when(ties > 0.0)
        def _():
            store(_route_tables(s1, s2, exact=True)[0])

        return carry

    lax.fori_loop(0, PEER_HEADS, head, 0)


def _route_tables(s1, s2, exact):
    K = PEER_TOPK
    tm = s1.shape[1]
    kiota = lax.broadcasted_iota(jnp.int32, s1.shape, 0).astype(F32)
    a, rank1 = _extract_top(s1, kiota, K, exact)
    b, rank2 = _extract_top(s2, kiota, K, exact)

    bmat = jnp.concatenate(b, axis=0)
    jio = lax.broadcasted_iota(jnp.int32, (K, tm), 0).astype(F32)
    blocks, keys = [], []
    half = K // 2
    for i in range(half):
        blocks.append(jnp.where(jio < float(K // (i + 1)), a[i] + bmat, -jnp.inf))
        keys.append(jio + float(i * K))
    blocks.append(jnp.concatenate([a[i] + b[0] for i in range(half, K)], axis=0))
    keys.append((lax.broadcasted_iota(jnp.int32, (K - half, tm), 0).astype(F32) + float(half)) * float(K))
    cand = jnp.concatenate(blocks, axis=0)
    ckey = jnp.concatenate(keys, axis=0)
    _, crank = _extract_top(cand, ckey, K, exact)
    sel = crank < float(K)
    z = jnp.sum(jnp.where(sel, jnp.exp(cand - (a[0] + b[0])), 0.0), axis=0, keepdims=True)
    self32 = sel.astype(F32)
    counts = [jnp.sum(self32[i * K:(i + 1) * K], axis=0, keepdims=True) for i in range(half)]
    counts += [self32[half * K + i:half * K + i + 1] for i in range(K - half)]

    cnt = jnp.zeros(s1.shape, F32)
    for i in range(K):
        cnt = jnp.where(rank1 == float(i), counts[i], cnt)
    removed = (jnp.sum((rank1 < float(K)).astype(F32), axis=0, keepdims=True)
               + jnp.sum((rank2 < float(K)).astype(F32), axis=0, keepdims=True)
               + jnp.sum(self32, axis=0, keepdims=True))
    return (rank2, jnp.exp(s2 - b[0]), cnt, jnp.exp(s1 - a[0]) / z), removed


def _peer_route(w_pq_t, n3t, keys):
    d, t = n3t.shape
    tm = _pick(t, (512, 256, 128))
    qw = PEER_HEADS * 2 * PEER_HALF
    spec = pl.BlockSpec((PEER_HEADS, 2, PEER_NKEYS, tm), lambda i: (0, 0, 0, i))
    return pl.pallas_call(
        _peer_route_kernel,
        grid=(t // tm,),
        in_specs=[
            pl.BlockSpec((qw, d), lambda i: (0, 0), pipeline_mode=pl.Buffered(1)),
            pl.BlockSpec((d, tm), lambda i: (0, i)),
            pl.BlockSpec((2, PEER_NKEYS, PEER_HALF), lambda i: (0, 0, 0)),
        ],
        out_specs=[spec, spec],
        out_shape=[jax.ShapeDtypeStruct((PEER_HEADS, 2, PEER_NKEYS, t), BF16),
                   jax.ShapeDtypeStruct((PEER_HEADS, 2, PEER_NKEYS, t), F32)],
        scratch_shapes=[pltpu.VMEM((qw, tm), F32)],
        compiler_params=_params("parallel"),
        name="peer_route",
    )(w_pq_t, n3t, keys)


def _gelu_tanh(x):
    inner = x * (0.7978845608028654 + (0.7978845608028654 * 0.044715) * (x * x))
    half = 0.5 * x
    return half + half * jnp.tanh(inner)


def _peer_kernel(n3t_ref, pk2_ref, pk1_ref, u_ref, vt_ref, o_ref, act_sc, *, ec, n_chunks):
    e = pl.program_id(1)
    cur = e % 2

    @pl.when(e == 0)
    def _():
        o_ref[...] = jnp.zeros_like(o_ref)
        act_sc[1] = jnp.zeros(act_sc.shape[1:], act_sc.dtype)

    h = jnp.dot(u_ref[...], n3t_ref[...], preferred_element_type=F32)
    o_ref[...] += jnp.dot(vt_ref[0], act_sc[1 - cur], preferred_element_type=F32)
    k1_base = jnp.minimum(e, n_chunks - 1) * (ec // PEER_NKEYS)
    for kk in range(ec // PEER_NKEYS):
        rows = slice(kk * PEER_NKEYS, (kk + 1) * PEER_NKEYS)
        gate = jnp.zeros((PEER_NKEYS, h.shape[1]), BF16)
        for hd in range(PEER_HEADS):
            cnt = pk1_ref[hd, 0, pl.ds(k1_base + kk, 1), :].astype(BF16)
            e1 = pk1_ref[hd, 1, pl.ds(k1_base + kk, 1), :].astype(BF16)
            gate = gate + jnp.where(pk2_ref[hd, 0] < cnt, pk2_ref[hd, 1], 0.0) * e1
        act_sc[cur, rows, :] = (_gelu_tanh(h[rows]) * gate.astype(F32)).astype(BF16)


def _peer(n3t, pk2, pk1, u, vt):
    d, t = n3t.shape
    n_chunks, _, ec = vt.shape
    tm = _pick(t, (512, 256, 128))
    once = pl.Buffered(1)
    return pl.pallas_call(
        functools.partial(_peer_kernel, ec=ec, n_chunks=n_chunks),
        grid=(t // tm, n_chunks + 1),
        in_specs=[
            pl.BlockSpec((d, tm), lambda i, e: (0, i), pipeline_mode=once),
            pl.BlockSpec((PEER_HEADS, 2, PEER_NKEYS, tm), lambda i, e: (0, 0, 0, i), pipeline_mode=once),
            pl.BlockSpec((PEER_HEADS, 2, PEER_NKEYS, tm), lambda i, e: (0, 0, 0, i), pipeline_mode=once),
            pl.BlockSpec((ec, d), lambda i, e: (jnp.minimum(e, n_chunks - 1), 0)),
            pl.BlockSpec((1, d, ec), lambda i, e: (jnp.maximum(e - 1, 0), 0, 0)),
        ],
        out_specs=pl.BlockSpec((d, tm), lambda i, e: (0, i)),
        out_shape=jax.ShapeDtypeStruct((d, t), F32),
        scratch_shapes=[pltpu.VMEM((2, ec, tm), BF16)],
        compiler_params=_params("parallel", "arbitrary"),
        name="peer_experts",
    )(n3t, pk2, pk1, u, vt)


def _final_kernel(h_ref, pt_ref, g_ref, o_ref):
    x = h_ref[...] + pt_ref[...].T
    y = x * lax.rsqrt(jnp.mean(x * x, axis=-1, keepdims=True) + EPS)
    o_ref[...] = y * g_ref[...]


def _final(h2, peer_t, g):
    t, d = h2.shape
    tm = _pick(t, (256, 128))
    return pl.pallas_call(
        _final_kernel,
        grid=(t // tm,),
        in_specs=[pl.BlockSpec((tm, d), lambda i: (i, 0)), pl.BlockSpec((d, tm), lambda i: (0, i)),
                  pl.BlockSpec((1, d), lambda i: (0, 0))],
        out_specs=pl.BlockSpec((tm, d), lambda i: (i, 0)),
        out_shape=jax.ShapeDtypeStruct((t, d), F32),
        compiler_params=_params("parallel"),
        name="final_norm",
    )(h2, peer_t, g)


def _layer(x, w, mem_k, mem_v, batch, tokens, n_mem, past):
    n = _rmsnorm(x, w["norm_mix"])
    z32, z16 = _mm(n, w["w_main"], (F32, BF16))
    small = _small_proj(n, w["w_small"], w["b_fgate"])
    fk = z32[:, COL_FK:COL_FK + FOX_WIDTH]
    fv = z32[:, COL_FV:COL_FV + FOX_WIDTH]
    logf = small[:, :FOX_HEADS]

    if past is None:
        lb = _pick(tokens, (512, 256, 128))
        c = _cumsum_t(small.reshape(batch, tokens, LANES), lb)
        tq = _pick(tokens, (512, 256, 128))
        c4 = c.reshape(batch * LANES, tokens // tq, 1, tq)
        fo = _fox_prompt(z16, c4, batch, tokens)
        go, gla_state = _gla(z32, z16, small, w["wa_pad"], w["b_gla_a"], w["norm_gla_out"], None, batch, tokens)
    else:
        ck, cv, clogf, s0 = past
        plen = ck.shape[1]
        lf = jnp.concatenate([
            jnp.pad(clogf.astype(F32), ((0, 0), (0, 0), (0, LANES - FOX_HEADS))),
            small.reshape(batch, tokens, LANES),
            jnp.zeros((batch, LANES - tokens, LANES), F32)], axis=1)
        c = _cumsum_t(lf, LANES)
        c3 = c.reshape(batch * LANES, 1, plen + LANES)
        fo = _fox_sample(z16, ck.reshape(batch * plen, FOX_WIDTH), cv.reshape(batch * plen, FOX_WIDTH), c3,
                         batch, tokens, plen)
        go, gla_state = _gla(z32, z16, small, w["wa_pad"], w["b_gla_a"], w["norm_gla_out"], s0.astype(F32),
                             batch, tokens)

    h1 = _outproj(fo, go, w["w_out"], x)
    h2, n3t = _mem_block(h1, w["norm_mem_q"], w["w_mq"], mem_k, mem_v, w["w_mo"], w["norm_ffn"], tokens, n_mem)
    pk2, pk1 = _peer_route(w["w_pq_t"], n3t, w["sub_keys"])
    peer_t = _peer(n3t, pk2, pk1, w["expert_u"], w["expert_v_t"])
    return h2, peer_t, fk, fv, logf, gla_state


def kernel(x_prompt, x_sample, cache_fox_k, cache_fox_v, cache_fox_logf, state_gla, cache_mem_k, cache_mem_v, mem_prompt, norm_mix, w_in, b_fgate, w_gla_a2, b_gla_a, norm_gla_out, w_out, norm_mem_q, norm_mem_kv, w_mq, w_mk, w_mv, w_mo, norm_ffn, w_pq, sub_keys, expert_u, expert_v, norm_final):
    depth = w_in.shape[0]
    assert depth == 1, "one trunk layer"
    bp, sp, d = x_prompt.shape
    bs, ss, _ = x_sample.shape
    n_mem = mem_prompt.shape[1]
    l = 0

    wi = w_in[l]
    o_ff = 3 * FOX_WIDTH
    o_g = o_ff + FOX_HEADS
    o_ga = o_g + 2 * GLA_KEY_WIDTH + 2 * GLA_WIDTH
    w = {
        "norm_mix": norm_mix[l],
        "w_main": jnp.concatenate([wi[:, :o_ff], wi[:, o_g:o_ga]], axis=1).astype(BF16),
        "w_small": jnp.concatenate([wi[:, o_ff:o_g], wi[:, o_ga:],
                                    jnp.zeros((d, LANES - FOX_HEADS - GLA_GATE_RANK), F32)], axis=1).astype(BF16),
        "b_fgate": jnp.pad(b_fgate[l], (0, LANES - FOX_HEADS)).reshape(1, LANES),
        "wa_pad": jnp.pad(w_gla_a2[l], ((FOX_HEADS, LANES - FOX_HEADS - GLA_GATE_RANK), (0, 0))).astype(BF16),
        "b_gla_a": b_gla_a[l].reshape(1, GLA_KEY_WIDTH),
        "norm_gla_out": norm_gla_out[l].reshape(1, GLA_DV),
        "w_out": w_out[l].astype(BF16),
        "norm_mem_q": norm_mem_q[l].reshape(1, d),
        "w_mq": w_mq[l].astype(BF16),
        "w_mo": w_mo[l].astype(BF16),
        "norm_ffn": norm_ffn[l].reshape(1, d),
        "w_pq_t": w_pq[l].T.astype(BF16),
        "sub_keys": sub_keys[l].astype(BF16),
        "expert_u": expert_u[l].astype(BF16),
        "expert_v_t": expert_v[l].reshape(-1, PEER_EXPERT_CHUNK, d).transpose(0, 2, 1).astype(BF16),
    }

    m = _rmsnorm(mem_prompt.reshape(bp * n_mem, d), norm_mem_kv[l])
    (mk,) = _mm(m, w_mk[l].astype(BF16), (F32,))
    (mv,) = _mm(m, w_mv[l].astype(BF16), (F32,))

    hp, pp, fkp, fvp, lfp, gsp = _layer(x_prompt.reshape(bp * sp, d), w, mk, mv, bp, sp, n_mem, None)
    past = (cache_fox_k[l], cache_fox_v[l], cache_fox_logf[l], state_gla[l])
    hs, ps, fks, fvs, lfs, gss = _layer(x_sample.reshape(bs * ss, d), w,
                                        cache_mem_k[l].reshape(bs * n_mem, MEM_WIDTH),
                                        cache_mem_v[l].reshape(bs * n_mem, MEM_WIDTH), bs, ss, n_mem, past)

    g_fin = norm_final.reshape(1, d)
    y_prompt = _final(hp, pp, g_fin).reshape(bp, sp, d)
    y_sample = _final(hs, ps, g_fin).reshape(bs, ss, d)
    hshape_p = (1, bp, sp, FOX_HEADS, FOX_HEAD_DIM)
    hshape_s = (1, bs, ss, FOX_HEADS, FOX_HEAD_DIM)
    return (y_prompt, y_sample,
            fkp.reshape(hshape_p), fvp.reshape(hshape_p), lfp.reshape(1, bp, sp, FOX_HEADS), gsp[None],
            mk.reshape(1, bp, n_mem, MEM_HEADS, MEM_HEAD_DIM), mv.reshape(1, bp, n_mem, MEM_HEADS, MEM_HEAD_DIM),
            fks.reshape(hshape_s), fvs.reshape(hshape_s), lfs.reshape(1, bs, ss, FOX_HEADS), gss[None])
```

```python
import functools

import jax
import jax.numpy as jnp
from jax import lax
from jax.experimental import pallas as pl
from jax.experimental.pallas import tpu as pltpu

F32 = jnp.float32
BF16 = jnp.bfloat16

EPS = 1e-6
NEG_INF = -1e30

FOX_HEADS = 16
FOX_HEAD_DIM = 128
FOX_WIDTH = FOX_HEADS * FOX_HEAD_DIM
FOX_HEADS_PER_STEP = 4
LOG2E = 1.4426950408889634
GLA_HEADS = 4
GLA_DK = 256
GLA_DV = 512
GLA_KEY_WIDTH = GLA_HEADS * GLA_DK
GLA_WIDTH = GLA_HEADS * GLA_DV
GLA_GATE_RANK = 16
GLA_TAU = 16.0
GLA_CHUNK = 128
GLA_SUB = 32
GLA_HEADS_PER_STEP = 4
MEM_HEADS = 4
MEM_HEAD_DIM = 128
MEM_WIDTH = MEM_HEADS * MEM_HEAD_DIM
PEER_HEADS = 8
PEER_NKEYS = 128
PEER_HALF = 128
PEER_TOPK = 16
PEER_EXPERT_CHUNK = 512
LANES = 128
GLA_PROJ_WIDTH = 2 * GLA_KEY_WIDTH + 2 * GLA_WIDTH

COL_FQ, COL_FK, COL_FV = 0, FOX_WIDTH, 2 * FOX_WIDTH
COL_GQ = 0
COL_GK = COL_GQ + GLA_KEY_WIDTH
COL_GV = COL_GK + GLA_KEY_WIDTH
COL_GG = COL_GV + GLA_WIDTH

VMEM_LIMIT_BYTES = 52 * 2**20


def _pick(n, prefs):
    for p in prefs:
        if n % p == 0:
            return p
    raise ValueError(f"no tile in {prefs} divides {n}")


def _params(*sem):
    return pltpu.CompilerParams(dimension_semantics=sem, vmem_limit_bytes=VMEM_LIMIT_BYTES)


def _log_sigmoid(x):
    return -(jnp.maximum(-x, 0.0) + jnp.log1p(jnp.exp(-jnp.abs(x))))


def _split3(x):
    hi = x.astype(BF16)
    r1 = x - hi.astype(F32)
    mid = r1.astype(BF16)
    lo = (r1 - mid.astype(F32)).astype(BF16)
    return hi, mid, lo


def _rmsnorm_kernel(x_ref, g_ref, o_ref):
    x = x_ref[...]
    y = x * lax.rsqrt(jnp.mean(x * x, axis=-1, keepdims=True) + EPS)
    o_ref[...] = (y * g_ref[...]).astype(o_ref.dtype)


def _rmsnorm(x, g):
    t, d = x.shape
    tm = _pick(t, (512, 256, 128))
    return pl.pallas_call(
        _rmsnorm_kernel,
        grid=(t // tm,),
        in_specs=[pl.BlockSpec((tm, d), lambda i: (i, 0)), pl.BlockSpec((1, d), lambda i: (0, 0))],
        out_specs=pl.BlockSpec((tm, d), lambda i: (i, 0)),
        out_shape=jax.ShapeDtypeStruct((t, d), BF16),
        compiler_params=_params("parallel"),
        name="rmsnorm",
    )(x, g.reshape(1, d))


def _mm_kernel(a_ref, w_ref, *o_refs):
    r = jnp.dot(a_ref[...], w_ref[...], preferred_element_type=F32)
    for o_ref in o_refs:
        o_ref[...] = r.astype(o_ref.dtype)


def _mm(a, w, out_dtypes, col0=0, ncols=None):
    t, k = a.shape
    n = w.shape[1] - col0 if ncols is None else ncols
    tm = _pick(t, (512, 256, 128))
    tn = _pick(n, (1024, 512, 256, 128))
    assert col0 % tn == 0
    j0 = col0 // tn
    outs = pl.pallas_call(
        _mm_kernel,
        grid=(n // tn, t // tm),
        in_specs=[pl.BlockSpec((tm, k), lambda j, i: (i, 0)), pl.BlockSpec((k, tn), lambda j, i: (0, j0 + j))],
        out_specs=[pl.BlockSpec((tm, tn), lambda j, i: (i, j)) for _ in out_dtypes],
        out_shape=[jax.ShapeDtypeStruct((t, n), dt) for dt in out_dtypes],
        compiler_params=_params("parallel", "parallel"),
        name="proj",
    )(a, w)
    return outs


def _small_kernel(n_ref, w_ref, b_ref, o_ref):
    z = jnp.dot(n_ref[...], w_ref[...], preferred_element_type=F32)
    col = lax.broadcasted_iota(jnp.int32, z.shape, 1)
    o_ref[...] = jnp.where(col < FOX_HEADS, _log_sigmoid(z + b_ref[...]), z)


def _small_proj(n, w_small, b_fgate_pad):
    t, k = n.shape
    tm = _pick(t, (512, 256, 128))
    return pl.pallas_call(
        _small_kernel,
        grid=(t // tm,),
        in_specs=[pl.BlockSpec((tm, k), lambda i: (i, 0)), pl.BlockSpec((k, LANES), lambda i: (0, 0)),
                  pl.BlockSpec((1, LANES), lambda i: (0, 0))],
        out_specs=pl.BlockSpec((tm, LANES), lambda i: (i, 0)),
        out_shape=jax.ShapeDtypeStruct((t, LANES), F32),
        compiler_params=_params("parallel"),
        name="small_proj",
    )(n, w_small, b_fgate_pad)


def _cumsum_kernel(x_ref, o_ref, carry_ref, *, lb):
    @pl.when(pl.program_id(1) == 0)
    def _():
        carry_ref[...] = jnp.zeros_like(carry_ref)

    xt = x_ref[0].T
    r = lax.broadcasted_iota(jnp.int32, (lb, lb), 0)
    c = lax.broadcasted_iota(jnp.int32, (lb, lb), 1)
    tri = (r <= c).astype(BF16)
    acc = carry_ref[...]
    for part in _split3(xt):
        acc = acc + jnp.dot(part, tri, preferred_element_type=F32)
    o_ref[0] = acc
    carry_ref[...] = jnp.broadcast_to(acc[:, lb - 1:lb], carry_ref.shape)


def _cumsum_t(x, lb):
    b, l, _ = x.shape
    return pl.pallas_call(
        functools.partial(_cumsum_kernel, lb=lb),
        grid=(b, l // lb),
        in_specs=[pl.BlockSpec((1, lb, LANES), lambda i, j: (i, j, 0))],
        out_specs=pl.BlockSpec((1, LANES, lb), lambda i, j: (i, 0, j)),
        out_shape=jax.ShapeDtypeStruct((b, LANES, l), F32),
        scratch_shapes=[pltpu.VMEM((LANES, lb), F32)],
        compiler_params=_params("parallel", "arbitrary"),
        name="cumsum_logf",
    )(x)


def _fox_prompt_kernel(q_ref, k_ref, v_ref, c_ref, o_ref, m_sc, acc_sc, *, tq, hb):
    qi = pl.program_id(2)
    qscale = FOX_HEAD_DIM ** -0.5 * LOG2E
    m_sc[...] = jnp.full_like(m_sc, NEG_INF)
    acc_sc[...] = jnp.zeros_like(acc_sc)

    def chunk(kc, masked):
        ks = pl.multiple_of(kc * tq, tq)
        ones = jnp.ones((tq, LANES), BF16)
        for hh in range(hb):
            cs = slice(hh * FOX_HEAD_DIM, (hh + 1) * FOX_HEAD_DIM)
            s = lax.dot_general(q_ref[:, cs], k_ref[pl.ds(ks, tq), cs], (((1,), (1,)), ((), ())),
                                preferred_element_type=F32) * qscale - c_ref[hh, kc] * LOG2E
            if masked:
                r = lax.broadcasted_iota(jnp.int32, s.shape, 0)
                c = lax.broadcasted_iota(jnp.int32, s.shape, 1)
                s = jnp.where(r >= c, s, NEG_INF)
            m_prev = m_sc[hh]
            m_new = jnp.maximum(m_prev, jnp.max(s, axis=1, keepdims=True))
            alpha = jnp.exp2(m_prev - m_new)
            p = jnp.exp2(s - jnp.concatenate([m_new] * (tq // LANES), axis=1))
            v_ext = jnp.concatenate([v_ref[pl.ds(ks, tq), cs], ones], axis=1)
            pv = jnp.dot(p.astype(BF16), v_ext, preferred_element_type=F32)
            acc_sc[hh] = jnp.concatenate([alpha, alpha], axis=1) * acc_sc[hh] + pv
            m_sc[hh] = m_new

    def body(kc, carry):
        chunk(kc, False)
        return carry

    lax.fori_loop(0, qi, body, 0)
    chunk(qi, True)
    for hh in range(hb):
        acc = acc_sc[hh]
        o_ref[:, hh * FOX_HEAD_DIM:(hh + 1) * FOX_HEAD_DIM] = (
            acc[:, :FOX_HEAD_DIM] / acc[:, FOX_HEAD_DIM:]).astype(o_ref.dtype)


def _fox_prompt(q16, k16, v16, c4, batch, seq):
    tq = _pick(seq, (512, 256, 128))
    nq = seq // tq
    hb = FOX_HEADS_PER_STEP
    wid = hb * FOX_HEAD_DIM
    return pl.pallas_call(
        functools.partial(_fox_prompt_kernel, tq=tq, hb=hb),
        grid=(batch, FOX_HEADS // hb, nq),
        in_specs=[
            pl.BlockSpec((tq, wid), lambda b, h, i: (b * nq + i, h)),
            pl.BlockSpec((seq, wid), lambda b, h, i: (b, h)),
            pl.BlockSpec((seq, wid), lambda b, h, i: (b, h)),
            pl.BlockSpec((hb, nq, 1, tq), lambda b, h, i: (b * (LANES // hb) + h, 0, 0, 0)),
        ],
        out_specs=pl.BlockSpec((tq, wid), lambda b, h, i: (b * nq + i, h)),
        out_shape=jax.ShapeDtypeStruct((batch * seq, FOX_WIDTH), BF16),
        scratch_shapes=[pltpu.VMEM((hb, tq, LANES), F32), pltpu.VMEM((hb, tq, 2 * FOX_HEAD_DIM), F32)],
        compiler_params=_params("parallel", "parallel", "arbitrary"),
        name="fox_prompt",
    )(q16, k16, v16, c4)


def _fox_sample_kernel(q_ref, kn_ref, vn_ref, ck_ref, cv_ref, c_ref, o_ref, *, past, new):
    scale = FOX_HEAD_DIM ** -0.5
    q = q_ref[...]
    pad = jnp.zeros((LANES - new, FOX_HEAD_DIM), BF16)
    kn = jnp.concatenate([kn_ref[...], pad], axis=0)
    vn = jnp.concatenate([vn_ref[...], pad], axis=0)
    dn = (((1,), (1,)), ((), ()))
    c = c_ref[0]
    hd = pl.program_id(1)
    ck = ck_ref[0, pl.ds(hd, past, stride=FOX_HEADS), :].astype(BF16)
    cv = cv_ref[0, pl.ds(hd, past, stride=FOX_HEADS), :].astype(BF16)
    s1 = lax.dot_general(q, ck, dn, preferred_element_type=F32) * scale - c[:, :past]
    s2 = lax.dot_general(q, kn, dn, preferred_element_type=F32) * scale - c[:, past:]
    r = lax.broadcasted_iota(jnp.int32, s2.shape, 0)
    cc = lax.broadcasted_iota(jnp.int32, s2.shape, 1)
    s2 = jnp.where(r >= cc, s2, NEG_INF)
    m = jnp.maximum(jnp.max(s1, axis=1, keepdims=True), jnp.max(s2, axis=1, keepdims=True))
    p1 = jnp.exp(s1 - m)
    p2 = jnp.exp(s2 - m)
    l = jnp.sum(p1, axis=1, keepdims=True) + jnp.sum(p2, axis=1, keepdims=True)
    o = (jnp.dot(p1.astype(BF16), cv, preferred_element_type=F32)
         + jnp.dot(p2.astype(BF16), vn, preferred_element_type=F32))
    o_ref[...] = (o / l).astype(o_ref.dtype)


def _fox_sample(q16, k16, v16, ck, cv, c3, batch, new, past):
    cache = pl.BlockSpec((1, past * FOX_HEADS, FOX_HEAD_DIM), lambda b, h: (b, 0, 0))
    head = pl.BlockSpec((new, FOX_HEAD_DIM), lambda b, h: (b, h))
    return pl.pallas_call(
        functools.partial(_fox_sample_kernel, past=past, new=new),
        grid=(batch, FOX_HEADS),
        in_specs=[head, head, head, cache, cache,
                  pl.BlockSpec((1, 1, past + LANES), lambda b, h: (b * LANES + h, 0, 0))],
        out_specs=head,
        out_shape=jax.ShapeDtypeStruct((batch * new, FOX_WIDTH), BF16),
        compiler_params=_params("parallel", "arbitrary"),
        name="fox_sample",
    )(q16, k16, v16, ck, cv, c3)


def _gla_kernel(q_ref, k_ref, v_ref, gg_ref, sm_ref, wa_ref, ba_ref, g_ref, s0_ref, go_ref, sout_ref, s_sc,
                *, valid, has_init, hb):
    ci = pl.program_id(2)

    @pl.when(ci == 0)
    def _():
        if has_init:
            s_sc[...] = s0_ref[0]
        else:
            s_sc[...] = jnp.zeros_like(s_sc)

    for hh in range(hb):
        ks = slice(hh * GLA_DK, (hh + 1) * GLA_DK)
        vs = slice(hh * GLA_DV, (hh + 1) * GLA_DV)
        _gla_chunk(q_ref.at[:, ks], k_ref.at[:, ks], v_ref.at[:, vs], gg_ref.at[:, vs], sm_ref,
                   wa_ref.at[:, ks], ba_ref.at[:, ks], g_ref, s_sc.at[hh], go_ref.at[:, vs], valid)

    @pl.when(ci == pl.num_programs(2) - 1)
    def _():
        sout_ref[0] = s_sc[...]


def _gla_chunk(q_ref, k_ref, v_ref, gg_ref, sm_ref, wa_ref, ba_ref, g_ref, s_ref, go_ref, valid):
    C = GLA_CHUNK

    def rows(ref, dtype):
        x = ref[...].astype(dtype)
        if valid < C:
            x = jnp.concatenate([x, jnp.zeros((C - valid, x.shape[1]), dtype)], axis=0)
        return x

    row = lax.broadcasted_iota(jnp.int32, (C, C), 0)
    col = lax.broadcasted_iota(jnp.int32, (C, C), 1)

    a = jnp.dot(rows(sm_ref, BF16), wa_ref[...], preferred_element_type=F32) + ba_ref[...]
    lg = _log_sigmoid(a) * (1.0 / GLA_TAU)
    if valid < C:
        lg = jnp.where(lax.broadcasted_iota(jnp.int32, lg.shape, 0) < valid, lg, 0.0)
    low = (row >= col).astype(BF16)
    bc = jnp.zeros_like(lg)
    for part in _split3(lg):
        bc = bc + jnp.dot(low, part, preferred_element_type=F32)

    q = rows(q_ref, F32) * (GLA_DK ** -0.5)
    k = rows(k_ref, F32)
    v = rows(v_ref, BF16)
    s = s_ref[...]

    o_state = jnp.dot((q * jnp.exp(bc)).astype(BF16), s.astype(BF16), preferred_element_type=F32)
    krow = lax.broadcasted_iota(jnp.int32, (C, GLA_DK), 0)
    sub_row = lax.broadcasted_iota(jnp.int32, (GLA_SUB, C), 0)
    sub_col = lax.broadcasted_iota(jnp.int32, (GLA_SUB, C), 1)
    outs = []
    for i in range(C // GLA_SUB):
        r0, r1 = i * GLA_SUB, (i + 1) * GLA_SUB
        if valid <= r0:
            outs.append(o_state[r0:r1])
            continue
        base = bc[r0 - 1:r0] if i > 0 else jnp.zeros((1, GLA_DK), F32)
        qt = (q[r0:r1] * jnp.exp(bc[r0:r1] - base)).astype(BF16)
        kt = jnp.where(krow < r1, k * jnp.exp(base - bc), 0.0).astype(BF16)
        att = lax.dot_general(qt, kt, (((1,), (1,)), ((), ())), preferred_element_type=F32)
        att = jnp.where(sub_col <= sub_row + r0, att, 0.0)
        outs.append(o_state[r0:r1] + jnp.dot(att.astype(BF16), v, preferred_element_type=F32))
    o = jnp.concatenate(outs, axis=0)

    b_end = bc[C - 1:C]
    kd = k * jnp.exp(b_end - bc)
    decay = jnp.broadcast_to(jnp.exp(b_end), (LANES, GLA_DK)).T
    decay = jnp.concatenate([decay] * (GLA_DV // LANES), axis=1)
    s_new = decay * s + jnp.dot(kd.T.astype(BF16), v, preferred_element_type=F32)
    s_ref[...] = s_new

    o = o[:valid]
    on = o * lax.rsqrt(jnp.mean(o * o, axis=-1, keepdims=True) + EPS) * g_ref[...]
    gg = gg_ref[...]
    go_ref[...] = (on * (gg * jax.nn.sigmoid(gg))).astype(go_ref.dtype)


def _gla(z32, z16, small, wa_pad, b_a, g_out, s0, batch, tokens):
    if tokens >= GLA_CHUNK:
        assert tokens % GLA_CHUNK == 0
        blk, nch = GLA_CHUNK, tokens // GLA_CHUNK
    else:
        blk, nch = tokens, 1
    hb = GLA_HEADS_PER_STEP
    kw, vw = hb * GLA_DK, hb * GLA_DV
    has_init = s0 is not None
    if s0 is None:
        s0 = jnp.zeros((1, hb, GLA_DK, GLA_DV), F32)
        s0_map = lambda b, h, c: (0, 0, 0, 0)
    else:
        s0_map = lambda b, h, c: (b, h, 0, 0)
    qb, kb = COL_GQ // kw, COL_GK // kw
    vb, gb = COL_GV // vw, COL_GG // vw
    go, s_out = pl.pallas_call(
        functools.partial(_gla_kernel, valid=blk, has_init=has_init, hb=hb),
        grid=(batch, GLA_HEADS // hb, nch),
        in_specs=[
            pl.BlockSpec((blk, kw), lambda b, h, c: (b * nch + c, qb + h)),
            pl.BlockSpec((blk, kw), lambda b, h, c: (b * nch + c, kb + h)),
            pl.BlockSpec((blk, vw), lambda b, h, c: (b * nch + c, vb + h)),
            pl.BlockSpec((blk, vw), lambda b, h, c: (b * nch + c, gb + h)),
            pl.BlockSpec((blk, LANES), lambda b, h, c: (b * nch + c, 0)),
            pl.BlockSpec((LANES, kw), lambda b, h, c: (0, h)),
            pl.BlockSpec((1, kw), lambda b, h, c: (0, h)),
            pl.BlockSpec((1, GLA_DV), lambda b, h, c: (0, 0)),
            pl.BlockSpec((1, hb, GLA_DK, GLA_DV), s0_map),
        ],
        out_specs=[
            pl.BlockSpec((blk, vw), lambda b, h, c: (b * nch + c, h)),
            pl.BlockSpec((1, hb, GLA_DK, GLA_DV), lambda b, h, c: (b, h, 0, 0)),
        ],
        out_shape=[jax.ShapeDtypeStruct((batch * tokens, GLA_WIDTH), BF16),
                   jax.ShapeDtypeStruct((batch, GLA_HEADS, GLA_DK, GLA_DV), F32)],
        scratch_shapes=[pltpu.VMEM((hb, GLA_DK, GLA_DV), F32)],
        compiler_params=_params("parallel", "parallel", "arbitrary"),
        name="gla",
    )(z32, z32, z16, z32, small, wa_pad, b_a, g_out, s0)
    return go, s_out


def _outproj_kernel(fo_ref, go_ref, wt_ref, wb_ref, x_ref, o_ref):
    o_ref[...] = (x_ref[...] + jnp.dot(fo_ref[...], wt_ref[...], preferred_element_type=F32)
                  + jnp.dot(go_ref[...], wb_ref[...], preferred_element_type=F32))


def _outproj(fo, go, w_out, x):
    t, d = x.shape
    half = fo.shape[1]
    tm = _pick(t, (512, 256, 128))
    tn = _pick(d, (1024, 512))
    return pl.pallas_call(
        _outproj_kernel,
        grid=(d // tn, t // tm),
        in_specs=[
            pl.BlockSpec((tm, half), lambda j, i: (i, 0)),
            pl.BlockSpec((tm, half), lambda j, i: (i, 0)),
            pl.BlockSpec((half, tn), lambda j, i: (0, j)),
            pl.BlockSpec((half, tn), lambda j, i: (1, j)),
            pl.BlockSpec((tm, tn), lambda j, i: (i, j)),
        ],
        out_specs=pl.BlockSpec((tm, tn), lambda j, i: (i, j)),
        out_shape=jax.ShapeDtypeStruct((t, d), F32),
        compiler_params=_params("parallel", "parallel"),
        name="outproj",
    )(fo, go, w_out, w_out, x)


def _mem_kernel(h_ref, gq_ref, wq_ref, mk_ref, mv_ref, wo_ref, gf_ref, h2_ref, n3t_ref, *, per_batch, n_mem):
    h = h_ref[...]
    tm = h.shape[0]
    n2 = (h * lax.rsqrt(jnp.mean(h * h, axis=-1, keepdims=True) + EPS) * gq_ref[...]).astype(BF16)
    q = jnp.dot(n2, wq_ref[...], preferred_element_type=F32)
    scale = MEM_HEAD_DIM ** -0.5
    rows_per = min(per_batch, tm)
    parts = []
    for b in range(tm // rows_per):
        heads = []
        for hd in range(MEM_HEADS):
            cs = slice(hd * MEM_HEAD_DIM, (hd + 1) * MEM_HEAD_DIM)
            qh = q[b * rows_per:(b + 1) * rows_per, cs].astype(BF16)
            kh = mk_ref[b * n_mem:(b + 1) * n_mem, cs].astype(BF16)
            vh = mv_ref[b * n_mem:(b + 1) * n_mem, cs].astype(BF16)
            s = lax.dot_general(qh, kh, (((1,), (1,)), ((), ())), preferred_element_type=F32) * scale
            s = s - jnp.max(s, axis=1, keepdims=True)
            p = jnp.exp(s)
            p = p / jnp.sum(p, axis=1, keepdims=True)
            heads.append(jnp.dot(p.astype(BF16), vh, preferred_element_type=F32))
        parts.append(jnp.concatenate(heads, axis=1))
    o = jnp.concatenate(parts, axis=0).astype(BF16)
    h2 = h + jnp.dot(o, wo_ref[...], preferred_element_type=F32)
    h2_ref[...] = h2
    n3 = h2 * lax.rsqrt(jnp.mean(h2 * h2, axis=-1, keepdims=True) + EPS) * gf_ref[...]
    n3t_ref[...] = n3.T.astype(BF16)


def _mem_block(h1, g_q, w_mq, mk, mv, w_mo, g_ffn, per_batch, n_mem):
    t, d = h1.shape
    tm = _pick(t, (256, 128)) if per_batch >= 256 else LANES
    nb = max(tm // per_batch, 1)
    tiles_per_batch = max(per_batch // tm, 1)
    return pl.pallas_call(
        functools.partial(_mem_kernel, per_batch=per_batch, n_mem=n_mem),
        grid=(t // tm,),
        in_specs=[
            pl.BlockSpec((tm, d), lambda i: (i, 0)),
            pl.BlockSpec((1, d), lambda i: (0, 0)),
            pl.BlockSpec((d, MEM_WIDTH), lambda i: (0, 0)),
            pl.BlockSpec((nb * n_mem, MEM_WIDTH), lambda i: (i // tiles_per_batch, 0)),
            pl.BlockSpec((nb * n_mem, MEM_WIDTH), lambda i: (i // tiles_per_batch, 0)),
            pl.BlockSpec((MEM_WIDTH, d), lambda i: (0, 0)),
            pl.BlockSpec((1, d), lambda i: (0, 0)),
        ],
        out_specs=[pl.BlockSpec((tm, d), lambda i: (i, 0)), pl.BlockSpec((d, tm), lambda i: (0, i))],
        out_shape=[jax.ShapeDtypeStruct((t, d), F32), jax.ShapeDtypeStruct((d, t), BF16)],
        compiler_params=_params("parallel"),
        name="mem_attn",
    )(h1, g_q, w_mq, mk, mv, w_mo, g_ffn)


def _extract_top(vals, key, count, exact):
    cur = vals
    rank = jnp.full(vals.shape, float(count), F32)
    tops = []
    big = jnp.float32(2**30)
    for r in range(count):
        m = jnp.max(cur, axis=0, keepdims=True)
        hit = cur == m
        if exact:
            first = jnp.min(jnp.where(hit, key, big), axis=0, keepdims=True)
            hit = key == first
        rank = jnp.where(hit, float(r), rank)
        cur = jnp.where(hit, -jnp.inf, cur)
        tops.append(m)
    return tops, rank


def _peer_route_kernel(w_ref, n3t_ref, keys_ref, pk2_ref, pk1_ref, qp_sc):
    qp_sc[...] = jnp.dot(w_ref[...], n3t_ref[...], preferred_element_type=F32)
    qd = 2 * PEER_HALF

    def head(hd, carry):
        r0 = pl.multiple_of(hd * qd, qd)
        s1 = jnp.dot(keys_ref[0], qp_sc[pl.ds(r0, PEER_HALF), :].astype(BF16), preferred_element_type=F32)
        s2 = jnp.dot(keys_ref[1], qp_sc[pl.ds(r0 + PEER_HALF, PEER_HALF), :].astype(BF16),
                     preferred_element_type=F32)

        def store(tables):
            rank2, e2, cnt, e1 = tables
            pk2_ref[hd, 0] = rank2.astype(BF16)
            pk2_ref[hd, 1] = e2.astype(BF16)
            pk1_ref[hd, 0] = cnt
            pk1_ref[hd, 1] = e1

        tables, removed = _route_tables(s1, s2, exact=False)
        store(tables)
        ties = jnp.max(jnp.abs(removed - float(3 * PEER_TOPK)))

        @pl.when(ties > 0.0)
        def _():
            store(_route_tables(s1, s2, exact=True)[0])

        return carry

    lax.fori_loop(0, PEER_HEADS, head, 0)


def _route_tables(s1, s2, exact):
    K = PEER_TOPK
    tm = s1.shape[1]
    kiota = lax.broadcasted_iota(jnp.int32, s1.shape, 0).astype(F32)
    a, rank1 = _extract_top(s1, kiota, K, exact)
    b, rank2 = _extract_top(s2, kiota, K, exact)

    bmat = jnp.concatenate(b, axis=0)
    jio = lax.broadcasted_iota(jnp.int32, (K, tm), 0).astype(F32)
    blocks, keys = [], []
    half = K // 2
    for i in range(half):
        blocks.append(jnp.where(jio < float(K // (i + 1)), a[i] + bmat, -jnp.inf))
        keys.append(jio + float(i * K))
    blocks.append(jnp.concatenate([a[i] + b[0] for i in range(half, K)], axis=0))
    keys.append((lax.broadcasted_iota(jnp.int32, (K - half, tm), 0).astype(F32) + float(half)) * float(K))
    cand = jnp.concatenate(blocks, axis=0)
    ckey = jnp.concatenate(keys, axis=0)
    _, crank = _extract_top(cand, ckey, K, exact)
    sel = crank < float(K)
    z = jnp.sum(jnp.where(sel, jnp.exp(cand - (a[0] + b[0])), 0.0), axis=0, keepdims=True)
    self32 = sel.astype(F32)
    counts = [jnp.sum(self32[i * K:(i + 1) * K], axis=0, keepdims=True) for i in range(half)]
    counts += [self32[half * K + i:half * K + i + 1] for i in range(K - half)]

    cnt = jnp.zeros(s1.shape, F32)
    for i in range(K):
        cnt = jnp.where(rank1 == float(i), counts[i], cnt)
    removed = (jnp.sum((rank1 < float(K)).astype(F32), axis=0, keepdims=True)
               + jnp.sum((rank2 < float(K)).astype(F32), axis=0, keepdims=True)
               + jnp.sum(self32, axis=0, keepdims=True))
    return (rank2, jnp.exp(s2 - b[0]), cnt, jnp.exp(s1 - a[0]) / z), removed


def _peer_route(w_pq_t, n3t, keys):
    d, t = n3t.shape
    tm = _pick(t, (512, 256, 128))
    qw = PEER_HEADS * 2 * PEER_HALF
    spec = pl.BlockSpec((PEER_HEADS, 2, PEER_NKEYS, tm), lambda i: (0, 0, 0, i))
    return pl.pallas_call(
        _peer_route_kernel,
        grid=(t // tm,),
        in_specs=[
            pl.BlockSpec((qw, d), lambda i: (0, 0), pipeline_mode=pl.Buffered(1)),
            pl.BlockSpec((d, tm), lambda i: (0, i)),
            pl.BlockSpec((2, PEER_NKEYS, PEER_HALF), lambda i: (0, 0, 0)),
        ],
        out_specs=[spec, spec],
        out_shape=[jax.ShapeDtypeStruct((PEER_HEADS, 2, PEER_NKEYS, t), BF16),
                   jax.ShapeDtypeStruct((PEER_HEADS, 2, PEER_NKEYS, t), F32)],
        scratch_shapes=[pltpu.VMEM((qw, tm), F32)],
        compiler_params=_params("parallel"),
        name="peer_route",
    )(w_pq_t, n3t, keys)


def _gelu_tanh(x):
    inner = x * (0.7978845608028654 + (0.7978845608028654 * 0.044715) * (x * x))
    half = 0.5 * x
    return half + half * jnp.tanh(inner)


def _peer_kernel(n3t_ref, pk2_ref, pk1_ref, u_ref, vt_ref, o_ref, act_sc, *, ec, n_chunks):
    e = pl.program_id(1)
    cur = e % 2

    @pl.when(e == 0)
    def _():
        o_ref[...] = jnp.zeros_like(o_ref)
        act_sc[1] = jnp.zeros(act_sc.shape[1:], act_sc.dtype)

    h = jnp.dot(u_ref[...], n3t_ref[...], preferred_element_type=F32)
    o_ref[...] += jnp.dot(vt_ref[0], act_sc[1 - cur], preferred_element_type=F32)
    k1_base = jnp.minimum(e, n_chunks - 1) * (ec // PEER_NKEYS)
    for kk in range(ec // PEER_NKEYS):
        rows = slice(kk * PEER_NKEYS, (kk + 1) * PEER_NKEYS)
        gate = jnp.zeros((PEER_NKEYS, h.shape[1]), BF16)
        for hd in range(PEER_HEADS):
            cnt = pk1_ref[hd, 0, pl.ds(k1_base + kk, 1), :].astype(BF16)
            e1 = pk1_ref[hd, 1, pl.ds(k1_base + kk, 1), :].astype(BF16)
            gate = gate + jnp.where(pk2_ref[hd, 0] < cnt, pk2_ref[hd, 1], 0.0) * e1
        act_sc[cur, rows, :] = (_gelu_tanh(h[rows]) * gate.astype(F32)).astype(BF16)


def _peer(n3t, pk2, pk1, u, vt):
    d, t = n3t.shape
    n_chunks, _, ec = vt.shape
    tm = _pick(t, (512, 256, 128))
    once = pl.Buffered(1)
    return pl.pallas_call(
        functools.partial(_peer_kernel, ec=ec, n_chunks=n_chunks),
        grid=(t // tm, n_chunks + 1),
        in_specs=[
            pl.BlockSpec((d, tm), lambda i, e: (0, i), pipeline_mode=once),
            pl.BlockSpec((PEER_HEADS, 2, PEER_NKEYS, tm), lambda i, e: (0, 0, 0, i), pipeline_mode=once),
            pl.BlockSpec((PEER_HEADS, 2, PEER_NKEYS, tm), lambda i, e: (0, 0, 0, i), pipeline_mode=once),
            pl.BlockSpec((ec, d), lambda i, e: (jnp.minimum(e, n_chunks - 1), 0)),
            pl.BlockSpec((1, d, ec), lambda i, e: (jnp.maximum(e - 1, 0), 0, 0)),
        ],
        out_specs=pl.BlockSpec((d, tm), lambda i, e: (0, i)),
        out_shape=jax.ShapeDtypeStruct((d, t), F32),
        scratch_shapes=[pltpu.VMEM((2, ec, tm), BF16)],
        compiler_params=_params("parallel", "arbitrary"),
        name="peer_experts",
    )(n3t, pk2, pk1, u, vt)


def _final_kernel(h_ref, pt_ref, g_ref, o_ref):
    x = h_ref[...] + pt_ref[...].T
    y = x * lax.rsqrt(jnp.mean(x * x, axis=-1, keepdims=True) + EPS)
    o_ref[...] = y * g_ref[...]


def _final(h2, peer_t, g):
    t, d = h2.shape
    tm = _pick(t, (256, 128))
    return pl.pallas_call(
        _final_kernel,
        grid=(t // tm,),
        in_specs=[pl.BlockSpec((tm, d), lambda i: (i, 0)), pl.BlockSpec((d, tm), lambda i: (0, i)),
                  pl.BlockSpec((1, d), lambda i: (0, 0))],
        out_specs=pl.BlockSpec((tm, d), lambda i: (i, 0)),
        out_shape=jax.ShapeDtypeStruct((t, d), F32),
        compiler_params=_params("parallel"),
        name="final_norm",
    )(h2, peer_t, g)


def _layer(x, w, mem_k, mem_v, batch, tokens, n_mem, past):
    n = _rmsnorm(x, w["norm_mix"])
    (q16,) = _mm(n, w["w_fox"], (BF16,), COL_FQ, FOX_WIDTH)
    fk, k16 = _mm(n, w["w_fox"], (F32, BF16), COL_FK, FOX_WIDTH)
    fv, v16 = _mm(n, w["w_fox"], (F32, BF16), COL_FV, FOX_WIDTH)
    z32, z16 = _mm(n, w["w_gla"], (F32, BF16))
    small = _small_proj(n, w["w_small"], w["b_fgate"])
    logf = small[:, :FOX_HEADS]

    if past is None:
        lb = _pick(tokens, (512, 256, 128))
        c = _cumsum_t(small.reshape(batch, tokens, LANES), lb)
        tq = _pick(tokens, (512, 256, 128))
        c4 = c.reshape(batch * LANES, tokens // tq, 1, tq)
        fo = _fox_prompt(q16, k16, v16, c4, batch, tokens)
        go, gla_state = _gla(z32, z16, small, w["wa_pad"], w["b_gla_a"], w["norm_gla_out"], None, batch, tokens)
    else:
        ck, cv, clogf, s0 = past
        plen = ck.shape[1]
        lf = jnp.concatenate([
            jnp.pad(clogf.astype(F32), ((0, 0), (0, 0), (0, LANES - FOX_HEADS))),
            small.reshape(batch, tokens, LANES),
            jnp.zeros((batch, LANES - tokens, LANES), F32)], axis=1)
        c = _cumsum_t(lf, LANES)
        c3 = c.reshape(batch * LANES, 1, plen + LANES)
        fo = _fox_sample(q16, k16, v16, ck.reshape(batch, plen * FOX_HEADS, FOX_HEAD_DIM),
                         cv.reshape(batch, plen * FOX_HEADS, FOX_HEAD_DIM), c3, batch, tokens, plen)
        go, gla_state = _gla(z32, z16, small, w["wa_pad"], w["b_gla_a"], w["norm_gla_out"], s0.astype(F32),
                             batch, tokens)

    h1 = _outproj(fo, go, w["w_out"], x)
    h2, n3t = _mem_block(h1, w["norm_mem_q"], w["w_mq"], mem_k, mem_v, w["w_mo"], w["norm_ffn"], tokens, n_mem)
    pk2, pk1 = _peer_route(w["w_pq_t"], n3t, w["sub_keys"])
    peer_t = _peer(n3t, pk2, pk1, w["expert_u"], w["expert_v_t"])
    return h2, peer_t, fk, fv, logf, gla_state


def kernel(x_prompt, x_sample, cache_fox_k, cache_fox_v, cache_fox_logf, state_gla, cache_mem_k, cache_mem_v, mem_prompt, norm_mix, w_in, b_fgate, w_gla_a2, b_gla_a, norm_gla_out, w_out, norm_mem_q, norm_mem_kv, w_mq, w_mk, w_mv, w_mo, norm_ffn, w_pq, sub_keys, expert_u, expert_v, norm_final):
    depth = w_in.shape[0]
    assert depth == 1, "one trunk layer"
    bp, sp, d = x_prompt.shape
    bs, ss, _ = x_sample.shape
    n_mem = mem_prompt.shape[1]
    l = 0

    wi = w_in[l]
    o_ff = 3 * FOX_WIDTH
    o_g = o_ff + FOX_HEADS
    o_ga = o_g + 2 * GLA_KEY_WIDTH + 2 * GLA_WIDTH
    w = {
        "norm_mix": norm_mix[l],
        "w_fox": wi[:, :o_ff].astype(BF16),
        "w_gla": wi[:, o_g:o_ga].astype(BF16),
        "w_small": jnp.concatenate([wi[:, o_ff:o_g], wi[:, o_ga:],
                                    jnp.zeros((d, LANES - FOX_HEADS - GLA_GATE_RANK), F32)], axis=1).astype(BF16),
        "b_fgate": jnp.pad(b_fgate[l], (0, LANES - FOX_HEADS)).reshape(1, LANES),
        "wa_pad": jnp.pad(w_gla_a2[l], ((FOX_HEADS, LANES - FOX_HEADS - GLA_GATE_RANK), (0, 0))).astype(BF16),
        "b_gla_a": b_gla_a[l].reshape(1, GLA_KEY_WIDTH),
        "norm_gla_out": norm_gla_out[l].reshape(1, GLA_DV),
        "w_out": w_out[l].astype(BF16),
        "norm_mem_q": norm_mem_q[l].reshape(1, d),
        "w_mq": w_mq[l].astype(BF16),
        "w_mo": w_mo[l].astype(BF16),
        "norm_ffn": norm_ffn[l].reshape(1, d),
        "w_pq_t": w_pq[l].T.astype(BF16),
        "sub_keys": sub_keys[l].astype(BF16),
        "expert_u": expert_u[l].astype(BF16),
        "expert_v_t": expert_v[l].reshape(-1, PEER_EXPERT_CHUNK, d).transpose(0, 2, 1).astype(BF16),
    }

    m = _rmsnorm(mem_prompt.reshape(bp * n_mem, d), norm_mem_kv[l])
    (mk,) = _mm(m, w_mk[l].astype(BF16), (F32,))
    (mv,) = _mm(m, w_mv[l].astype(BF16), (F32,))

    hp, pp, fkp, fvp, lfp, gsp = _layer(x_prompt.reshape(bp * sp, d), w, mk, mv, bp, sp, n_mem, None)
    past = (cache_fox_k[l], cache_fox_v[l], cache_fox_logf[l], state_gla[l])
    hs, ps, fks, fvs, lfs, gss = _layer(x_sample.reshape(bs * ss, d), w,
                                        cache_mem_k[l].reshape(bs * n_mem, MEM_WIDTH),
                                        cache_mem_v[l].reshape(bs * n_mem, MEM_WIDTH), bs, ss, n_mem, past)

    g_fin = norm_final.reshape(1, d)
    y_prompt = _final(hp, pp, g_fin).reshape(bp, sp, d)
    y_sample = _final(hs, ps, g_fin).reshape(bs, ss, d)
    hshape_p = (1, bp, sp, FOX_HEADS, FOX_HEAD_DIM)
    hshape_s = (1, bs, ss, FOX_HEADS, FOX_HEAD_DIM)
    return (y_prompt, y_sample,
            fkp.reshape(hshape_p), fvp.reshape(hshape_p), lfp.reshape(1, bp, sp, FOX_HEADS), gsp[None],
            mk.reshape(1, bp, n_mem, MEM_HEADS, MEM_HEAD_DIM), mv.reshape(1, bp, n_mem, MEM_HEADS, MEM_HEAD_DIM),
            fks.reshape(hshape_s), fvs.reshape(hshape_s), lfs.reshape(1, bs, ss, FOX_HEADS), gss[None])
```

```python
import functools

import jax
import jax.numpy as jnp
from jax import lax
from jax.experimental import pallas as pl
from jax.experimental.pallas import tpu as pltpu

F32 = jnp.float32
BF16 = jnp.bfloat16

EPS = 1e-6
NEG_INF = -1e30

FOX_HEADS = 16
FOX_HEAD_DIM = 128
FOX_WIDTH = FOX_HEADS * FOX_HEAD_DIM
FOX_HEADS_PER_STEP = 4
LOG2E = 1.4426950408889634
FOX_QSCALE = FOX_HEAD_DIM ** -0.5 * LOG2E
GLA_HEADS = 4
GLA_DK = 256
GLA_DV = 512
GLA_KEY_WIDTH = GLA_HEADS * GLA_DK
GLA_WIDTH = GLA_HEADS * GLA_DV
GLA_GATE_RANK = 16
GLA_TAU = 16.0
GLA_CHUNK = 128
GLA_SUB = 32
GLA_HEADS_PER_STEP = 4
MEM_HEADS = 4
MEM_HEAD_DIM = 128
MEM_WIDTH = MEM_HEADS * MEM_HEAD_DIM
PEER_HEADS = 8
PEER_NKEYS = 128
PEER_HALF = 128
PEER_TOPK = 16
PEER_EXPERT_CHUNK = 512
LANES = 128
GLA_PROJ_WIDTH = 2 * GLA_KEY_WIDTH + 2 * GLA_WIDTH

COL_FQ, COL_FK, COL_FV = 0, FOX_WIDTH, 2 * FOX_WIDTH
COL_GQ = 0
COL_GK = COL_GQ + GLA_KEY_WIDTH
COL_GV = COL_GK + GLA_KEY_WIDTH
COL_GG = COL_GV + GLA_WIDTH

VMEM_LIMIT_BYTES = 52 * 2**20


def _pick(n, prefs):
    for p in prefs:
        if n % p == 0:
            return p
    raise ValueError(f"no tile in {prefs} divides {n}")


def _params(*sem):
    return pltpu.CompilerParams(dimension_semantics=sem, vmem_limit_bytes=VMEM_LIMIT_BYTES)


def _log_sigmoid(x):
    return -(jnp.maximum(-x, 0.0) + jnp.log1p(jnp.exp(-jnp.abs(x))))


def _split3(x):
    hi = x.astype(BF16)
    r1 = x - hi.astype(F32)
    mid = r1.astype(BF16)
    lo = (r1 - mid.astype(F32)).astype(BF16)
    return hi, mid, lo


def _rmsnorm_kernel(x_ref, g_ref, o_ref):
    x = x_ref[...]
    y = x * lax.rsqrt(jnp.mean(x * x, axis=-1, keepdims=True) + EPS)
    o_ref[...] = (y * g_ref[...]).astype(o_ref.dtype)


def _rmsnorm(x, g):
    t, d = x.shape
    tm = _pick(t, (512, 256, 128))
    return pl.pallas_call(
        _rmsnorm_kernel,
        grid=(t // tm,),
        in_specs=[pl.BlockSpec((tm, d), lambda i: (i, 0)), pl.BlockSpec((1, d), lambda i: (0, 0))],
        out_specs=pl.BlockSpec((tm, d), lambda i: (i, 0)),
        out_shape=jax.ShapeDtypeStruct((t, d), BF16),
        compiler_params=_params("parallel"),
        name="rmsnorm",
    )(x, g.reshape(1, d))


def _mm_kernel(a_ref, w_ref, *o_refs, scale, w_rows):
    if w_rows:
        r = lax.dot_general(a_ref[...], w_ref[...], (((1,), (1,)), ((), ())), preferred_element_type=F32)
    else:
        r = jnp.dot(a_ref[...], w_ref[...], preferred_element_type=F32)
    if scale is not None:
        r = r * scale
    for o_ref in o_refs:
        o_ref[...] = r.astype(o_ref.dtype)


def _mm(a, w, out_dtypes, col0=0, ncols=None, scale=None, w_rows=False):
    t, k = a.shape
    n = w.shape[0 if w_rows else 1] - col0 if ncols is None else ncols
    tm = _pick(t, (512, 256, 128))
    tn = _pick(n, (1024, 512, 256, 128))
    assert col0 % tn == 0
    j0 = col0 // tn
    if w_rows:
        w_spec = pl.BlockSpec((tn, k), lambda j, i: (j0 + j, 0))
    else:
        w_spec = pl.BlockSpec((k, tn), lambda j, i: (0, j0 + j))
    outs = pl.pallas_call(
        functools.partial(_mm_kernel, scale=scale, w_rows=w_rows),
        grid=(n // tn, t // tm),
        in_specs=[pl.BlockSpec((tm, k), lambda j, i: (i, 0)), w_spec],
        out_specs=[pl.BlockSpec((tm, tn), lambda j, i: (i, j)) for _ in out_dtypes],
        out_shape=[jax.ShapeDtypeStruct((t, n), dt) for dt in out_dtypes],
        compiler_params=_params("parallel", "parallel"),
        name="proj",
    )(a, w)
    return outs


def _small_kernel(n_ref, w_ref, b_ref, o_ref):
    z = lax.dot_general(n_ref[...], w_ref[...], (((1,), (1,)), ((), ())), preferred_element_type=F32)
    col = lax.broadcasted_iota(jnp.int32, z.shape, 1)
    o_ref[...] = jnp.where(col < FOX_HEADS, _log_sigmoid(z + b_ref[...]), z)


def _small_proj(n, w_small_t, b_fgate_pad):
    t, k = n.shape
    tm = _pick(t, (512, 256, 128))
    return pl.pallas_call(
        _small_kernel,
        grid=(t // tm,),
        in_specs=[pl.BlockSpec((tm, k), lambda i: (i, 0)), pl.BlockSpec((LANES, k), lambda i: (0, 0)),
                  pl.BlockSpec((1, LANES), lambda i: (0, 0))],
        out_specs=pl.BlockSpec((tm, LANES), lambda i: (i, 0)),
        out_shape=jax.ShapeDtypeStruct((t, LANES), F32),
        compiler_params=_params("parallel"),
        name="small_proj",
    )(n, w_small_t, b_fgate_pad)


def _cumsum_kernel(x_ref, o_ref, carry_ref, *, lb):
    @pl.when(pl.program_id(1) == 0)
    def _():
        carry_ref[...] = jnp.zeros_like(carry_ref)

    xt = x_ref[0].T
    r = lax.broadcasted_iota(jnp.int32, (lb, lb), 0)
    c = lax.broadcasted_iota(jnp.int32, (lb, lb), 1)
    tri = (r <= c).astype(BF16)
    acc = carry_ref[...]
    for part in _split3(xt):
        acc = acc + jnp.dot(part, tri, preferred_element_type=F32)
    o_ref[0] = acc
    carry_ref[...] = jnp.broadcast_to(acc[:, lb - 1:lb], carry_ref.shape)


def _cumsum_t(x, lb):
    b, l, _ = x.shape
    return pl.pallas_call(
        functools.partial(_cumsum_kernel, lb=lb),
        grid=(b, l // lb),
        in_specs=[pl.BlockSpec((1, lb, LANES), lambda i, j: (i, j, 0))],
        out_specs=pl.BlockSpec((1, LANES, lb), lambda i, j: (i, 0, j)),
        out_shape=jax.ShapeDtypeStruct((b, LANES, l), F32),
        scratch_shapes=[pltpu.VMEM((LANES, lb), F32)],
        compiler_params=_params("parallel", "arbitrary"),
        name="cumsum_logf",
    )(x)


def _fox_prompt_kernel(q_ref, k_ref, v_ref, c_ref, o_ref, m_sc, acc_sc, *, tq, hb):
    qi = pl.program_id(2)
    m_sc[...] = jnp.full_like(m_sc, NEG_INF)
    acc_sc[...] = jnp.zeros_like(acc_sc)

    def chunk(kc, masked):
        ks = pl.multiple_of(kc * tq, tq)
        ones = jnp.ones((tq, LANES), BF16)
        for hh in range(hb):
            cs = slice(hh * FOX_HEAD_DIM, (hh + 1) * FOX_HEAD_DIM)
            s = lax.dot_general(q_ref[:, cs], k_ref[pl.ds(ks, tq), cs], (((1,), (1,)), ((), ())),
                                preferred_element_type=F32) - c_ref[hh, kc] * LOG2E
            if masked:
                r = lax.broadcasted_iota(jnp.int32, s.shape, 0)
                c = lax.broadcasted_iota(jnp.int32, s.shape, 1)
                s = jnp.where(r >= c, s, NEG_INF)
            m_prev = m_sc[hh]
            m_new = jnp.maximum(m_prev, jnp.max(s, axis=1, keepdims=True))
            alpha = jnp.exp2(m_prev - m_new)
            p = jnp.exp2(s - jnp.concatenate([m_new] * (tq // LANES), axis=1))
            v_ext = jnp.concatenate([v_ref[pl.ds(ks, tq), cs], ones], axis=1)
            pv = jnp.dot(p.astype(BF16), v_ext, preferred_element_type=F32)
            acc_sc[hh] = jnp.concatenate([alpha, alpha], axis=1) * acc_sc[hh] + pv
            m_sc[hh] = m_new

    def body(kc, carry):
        chunk(kc, False)
        return carry

    lax.fori_loop(0, qi, body, 0)
    chunk(qi, True)
    for hh in range(hb):
        acc = acc_sc[hh]
        o_ref[:, hh * FOX_HEAD_DIM:(hh + 1) * FOX_HEAD_DIM] = (
            acc[:, :FOX_HEAD_DIM] / acc[:, FOX_HEAD_DIM:]).astype(o_ref.dtype)


def _fox_prompt(q16, k16, v16, c4, batch, seq):
    tq = _pick(seq, (512, 256, 128))
    nq = seq // tq
    hb = FOX_HEADS_PER_STEP
    wid = hb * FOX_HEAD_DIM
    return pl.pallas_call(
        functools.partial(_fox_prompt_kernel, tq=tq, hb=hb),
        grid=(batch, FOX_HEADS // hb, nq),
        in_specs=[
            pl.BlockSpec((tq, wid), lambda b, h, i: (b * nq + i, h)),
            pl.BlockSpec((seq, wid), lambda b, h, i: (b, h)),
            pl.BlockSpec((seq, wid), lambda b, h, i: (b, h)),
            pl.BlockSpec((hb, nq, 1, tq), lambda b, h, i: (b * (LANES // hb) + h, 0, 0, 0)),
        ],
        out_specs=pl.BlockSpec((tq, wid), lambda b, h, i: (b * nq + i, h)),
        out_shape=jax.ShapeDtypeStruct((batch * seq, FOX_WIDTH), BF16),
        scratch_shapes=[pltpu.VMEM((hb, tq, LANES), F32), pltpu.VMEM((hb, tq, 2 * FOX_HEAD_DIM), F32)],
        compiler_params=_params("parallel", "parallel", "arbitrary"),
        name="fox_prompt",
    )(q16, k16, v16, c4)


def _fox_sample_kernel(q_ref, kn_ref, vn_ref, ck_ref, cv_ref, c_ref, o_ref, *, past, new):
    pad = jnp.zeros((LANES - new, FOX_HEAD_DIM), BF16)
    dn = (((1,), (1,)), ((), ()))
    r = lax.broadcasted_iota(jnp.int32, (new, LANES), 0)
    cc = lax.broadcasted_iota(jnp.int32, (new, LANES), 1)
    for hd in range(FOX_HEADS):
        cs = slice(hd * FOX_HEAD_DIM, (hd + 1) * FOX_HEAD_DIM)
        q = q_ref[:, cs]
        kn = jnp.concatenate([kn_ref[:, cs], pad], axis=0)
        vn = jnp.concatenate([vn_ref[:, cs], pad], axis=0)
        c = c_ref[hd] * LOG2E
        ck = ck_ref[0, pl.ds(hd, past, stride=FOX_HEADS), :].astype(BF16)
        cv = cv_ref[0, pl.ds(hd, past, stride=FOX_HEADS), :].astype(BF16)
        s1 = lax.dot_general(q, ck, dn, preferred_element_type=F32) - c[:, :past]
        s2 = lax.dot_general(q, kn, dn, preferred_element_type=F32) - c[:, past:]
        s2 = jnp.where(r >= cc, s2, NEG_INF)
        m = jnp.maximum(jnp.max(s1, axis=1, keepdims=True), jnp.max(s2, axis=1, keepdims=True))
        p1 = jnp.exp2(s1 - m)
        p2 = jnp.exp2(s2 - m)
        l = jnp.sum(p1, axis=1, keepdims=True) + jnp.sum(p2, axis=1, keepdims=True)
        o = (jnp.dot(p1.astype(BF16), cv, preferred_element_type=F32)
             + jnp.dot(p2.astype(BF16), vn, preferred_element_type=F32))
        o_ref[:, cs] = (o / l).astype(o_ref.dtype)


def _fox_sample(q16, k16, v16, ck, cv, c3, batch, new, past):
    cache = pl.BlockSpec((1, past * FOX_HEADS, FOX_HEAD_DIM), lambda b: (b, 0, 0))
    rows = pl.BlockSpec((new, FOX_WIDTH), lambda b: (b, 0))
    return pl.pallas_call(
        functools.partial(_fox_sample_kernel, past=past, new=new),
        grid=(batch,),
        in_specs=[rows, rows, rows, cache, cache,
                  pl.BlockSpec((FOX_HEADS, 1, past + LANES), lambda b: (b * (LANES // FOX_HEADS), 0, 0))],
        out_specs=rows,
        out_shape=jax.ShapeDtypeStruct((batch * new, FOX_WIDTH), BF16),
        compiler_params=_params("parallel"),
        name="fox_sample",
    )(q16, k16, v16, ck, cv, c3)


def _gla_kernel(q_ref, k_ref, v_ref, gg_ref, sm_ref, wa_ref, ba_ref, g_ref, s0_ref, go_ref, sout_ref, s_sc,
                *, valid, has_init, hb):
    ci = pl.program_id(2)

    @pl.when(ci == 0)
    def _():
        if has_init:
            s_sc[...] = s0_ref[0]
        else:
            s_sc[...] = jnp.zeros_like(s_sc)

    for hh in range(hb):
        ks = slice(hh * GLA_DK, (hh + 1) * GLA_DK)
        vs = slice(hh * GLA_DV, (hh + 1) * GLA_DV)
        _gla_chunk(q_ref.at[:, ks], k_ref.at[:, ks], v_ref.at[:, vs], gg_ref.at[:, vs], sm_ref,
                   wa_ref.at[:, ks], ba_ref.at[:, ks], g_ref, s_sc.at[hh], go_ref.at[:, vs], valid)

    @pl.when(ci == pl.num_programs(2) - 1)
    def _():
        sout_ref[0] = s_sc[...]


def _gla_chunk(q_ref, k_ref, v_ref, gg_ref, sm_ref, wa_ref, ba_ref, g_ref, s_ref, go_ref, valid):
    C = GLA_CHUNK

    def rows(ref, dtype):
        x = ref[...].astype(dtype)
        if valid < C:
            x = jnp.concatenate([x, jnp.zeros((C - valid, x.shape[1]), dtype)], axis=0)
        return x

    row = lax.broadcasted_iota(jnp.int32, (C, C), 0)
    col = lax.broadcasted_iota(jnp.int32, (C, C), 1)

    a = jnp.dot(rows(sm_ref, BF16), wa_ref[...], preferred_element_type=F32) + ba_ref[...]
    lg = _log_sigmoid(a) * (1.0 / GLA_TAU)
    if valid < C:
        lg = jnp.where(lax.broadcasted_iota(jnp.int32, lg.shape, 0) < valid, lg, 0.0)
    low = (row >= col).astype(BF16)
    bc = jnp.zeros_like(lg)
    for part in _split3(lg):
        bc = bc + jnp.dot(low, part, preferred_element_type=F32)

    q = rows(q_ref, F32) * (GLA_DK ** -0.5)
    k = rows(k_ref, F32)
    v = rows(v_ref, BF16)
    s = s_ref[...]

    o_state = jnp.dot((q * jnp.exp(bc)).astype(BF16), s.astype(BF16), preferred_element_type=F32)
    krow = lax.broadcasted_iota(jnp.int32, (C, GLA_DK), 0)
    sub_row = lax.broadcasted_iota(jnp.int32, (GLA_SUB, C), 0)
    sub_col = lax.broadcasted_iota(jnp.int32, (GLA_SUB, C), 1)
    outs = []
    for i in range(C // GLA_SUB):
        r0, r1 = i * GLA_SUB, (i + 1) * GLA_SUB
        if valid <= r0:
            outs.append(o_state[r0:r1])
            continue
        base = bc[r0 - 1:r0] if i > 0 else jnp.zeros((1, GLA_DK), F32)
        qt = (q[r0:r1] * jnp.exp(bc[r0:r1] - base)).astype(BF16)
        kt = jnp.where(krow < r1, k * jnp.exp(base - bc), 0.0).astype(BF16)
        att = lax.dot_general(qt, kt, (((1,), (1,)), ((), ())), preferred_element_type=F32)
        att = jnp.where(sub_col <= sub_row + r0, att, 0.0)
        outs.append(o_state[r0:r1] + jnp.dot(att.astype(BF16), v, preferred_element_type=F32))
    o = jnp.concatenate(outs, axis=0)

    b_end = bc[C - 1:C]
    kd = k * jnp.exp(b_end - bc)
    decay = jnp.broadcast_to(jnp.exp(b_end), (LANES, GLA_DK)).T
    decay = jnp.concatenate([decay] * (GLA_DV // LANES), axis=1)
    s_new = decay * s + jnp.dot(kd.T.astype(BF16), v, preferred_element_type=F32)
    s_ref[...] = s_new

    o = o[:valid]
    on = o * lax.rsqrt(jnp.mean(o * o, axis=-1, keepdims=True) + EPS) * g_ref[...]
    gg = gg_ref[...]
    go_ref[...] = (on * (gg * jax.nn.sigmoid(gg))).astype(go_ref.dtype)


def _gla(z32, z16, small, wa_pad, b_a, g_out, s0, batch, tokens):
    if tokens >= GLA_CHUNK:
        assert tokens % GLA_CHUNK == 0
        blk, nch = GLA_CHUNK, tokens // GLA_CHUNK
    else:
        blk, nch = tokens, 1
    hb = GLA_HEADS_PER_STEP
    kw, vw = hb * GLA_DK, hb * GLA_DV
    has_init = s0 is not None
    if s0 is None:
        s0 = jnp.zeros((1, hb, GLA_DK, GLA_DV), F32)
        s0_map = lambda b, h, c: (0, 0, 0, 0)
    else:
        s0_map = lambda b, h, c: (b, h, 0, 0)
    qb, kb = COL_GQ // kw, COL_GK // kw
    vb, gb = COL_GV // vw, COL_GG // vw
    go, s_out = pl.pallas_call(
        functools.partial(_gla_kernel, valid=blk, has_init=has_init, hb=hb),
        grid=(batch, GLA_HEADS // hb, nch),
        in_specs=[
            pl.BlockSpec((blk, kw), lambda b, h, c: (b * nch + c, qb + h)),
            pl.BlockSpec((blk, kw), lambda b, h, c: (b * nch + c, kb + h)),
            pl.BlockSpec((blk, vw), lambda b, h, c: (b * nch + c, vb + h)),
            pl.BlockSpec((blk, vw), lambda b, h, c: (b * nch + c, gb + h)),
            pl.BlockSpec((blk, LANES), lambda b, h, c: (b * nch + c, 0)),
            pl.BlockSpec((LANES, kw), lambda b, h, c: (0, h)),
            pl.BlockSpec((1, kw), lambda b, h, c: (0, h)),
            pl.BlockSpec((1, GLA_DV), lambda b, h, c: (0, 0)),
            pl.BlockSpec((1, hb, GLA_DK, GLA_DV), s0_map),
        ],
        out_specs=[
            pl.BlockSpec((blk, vw), lambda b, h, c: (b * nch + c, h)),
            pl.BlockSpec((1, hb, GLA_DK, GLA_DV), lambda b, h, c: (b, h, 0, 0)),
        ],
        out_shape=[jax.ShapeDtypeStruct((batch * tokens, GLA_WIDTH), BF16),
                   jax.ShapeDtypeStruct((batch, GLA_HEADS, GLA_DK, GLA_DV), F32)],
        scratch_shapes=[pltpu.VMEM((hb, GLA_DK, GLA_DV), F32)],
        compiler_params=_params("parallel", "parallel", "arbitrary"),
        name="gla",
    )(z32, z32, z16, z32, small, wa_pad, b_a, g_out, s0)
    return go, s_out


def _outproj_kernel(fo_ref, go_ref, wt_ref, wb_ref, x_ref, o_ref):
    o_ref[...] = (x_ref[...] + jnp.dot(fo_ref[...], wt_ref[...], preferred_element_type=F32)
                  + jnp.dot(go_ref[...], wb_ref[...], preferred_element_type=F32))


def _outproj(fo, go, w_out, x):
    t, d = x.shape
    half = fo.shape[1]
    tm = _pick(t, (512, 256, 128))
    tn = _pick(d, (1024, 512))
    return pl.pallas_call(
        _outproj_kernel,
        grid=(d // tn, t // tm),
        in_specs=[
            pl.BlockSpec((tm, half), lambda j, i: (i, 0)),
            pl.BlockSpec((tm, half), lambda j, i: (i, 0)),
            pl.BlockSpec((half, tn), lambda j, i: (0, j)),
            pl.BlockSpec((half, tn), lambda j, i: (1, j)),
            pl.BlockSpec((tm, tn), lambda j, i: (i, j)),
        ],
        out_specs=pl.BlockSpec((tm, tn), lambda j, i: (i, j)),
        out_shape=jax.ShapeDtypeStruct((t, d), F32),
        compiler_params=_params("parallel", "parallel"),
        name="outproj",
    )(fo, go, w_out, w_out, x)


def _mem_kernel(h_ref, gq_ref, wq_ref, mk_ref, mv_ref, wo_ref, gf_ref, h2_ref, n3t_ref, *, per_batch, n_mem):
    h = h_ref[...]
    tm = h.shape[0]
    n2 = (h * lax.rsqrt(jnp.mean(h * h, axis=-1, keepdims=True) + EPS) * gq_ref[...]).astype(BF16)
    q = jnp.dot(n2, wq_ref[...], preferred_element_type=F32)
    scale = MEM_HEAD_DIM ** -0.5
    rows_per = min(per_batch, tm)
    parts = []
    for b in range(tm // rows_per):
        heads = []
        for hd in range(MEM_HEADS):
            cs = slice(hd * MEM_HEAD_DIM, (hd + 1) * MEM_HEAD_DIM)
            qh = q[b * rows_per:(b + 1) * rows_per, cs].astype(BF16)
            kh = mk_ref[b * n_mem:(b + 1) * n_mem, cs].astype(BF16)
            vh = mv_ref[b * n_mem:(b + 1) * n_mem, cs].astype(BF16)
            s = lax.dot_general(qh, kh, (((1,), (1,)), ((), ())), preferred_element_type=F32) * scale
            s = s - jnp.max(s, axis=1, keepdims=True)
            p = jnp.exp(s)
            p = p / jnp.sum(p, axis=1, keepdims=True)
            heads.append(jnp.dot(p.astype(BF16), vh, preferred_element_type=F32))
        parts.append(jnp.concatenate(heads, axis=1))
    o = jnp.concatenate(parts, axis=0).astype(BF16)
    h2 = h + jnp.dot(o, wo_ref[...], preferred_element_type=F32)
    h2_ref[...] = h2
    n3 = h2 * lax.rsqrt(jnp.mean(h2 * h2, axis=-1, keepdims=True) + EPS) * gf_ref[...]
    n3t_ref[...] = n3.T.astype(BF16)


def _mem_block(h1, g_q, w_mq, mk, mv, w_mo, g_ffn, per_batch, n_mem):
    t, d = h1.shape
    tm = _pick(t, (256, 128)) if per_batch >= 256 else LANES
    nb = max(tm // per_batch, 1)
    tiles_per_batch = max(per_batch // tm, 1)
    return pl.pallas_call(
        functools.partial(_mem_kernel, per_batch=per_batch, n_mem=n_mem),
        grid=(t // tm,),
        in_specs=[
            pl.BlockSpec((tm, d), lambda i: (i, 0)),
            pl.BlockSpec((1, d), lambda i: (0, 0)),
            pl.BlockSpec((d, MEM_WIDTH), lambda i: (0, 0)),
            pl.BlockSpec((nb * n_mem, MEM_WIDTH), lambda i: (i // tiles_per_batch, 0)),
            pl.BlockSpec((nb * n_mem, MEM_WIDTH), lambda i: (i // tiles_per_batch, 0)),
            pl.BlockSpec((MEM_WIDTH, d), lambda i: (0, 0)),
            pl.BlockSpec((1, d), lambda i: (0, 0)),
        ],
        out_specs=[pl.BlockSpec((tm, d), lambda i: (i, 0)), pl.BlockSpec((d, tm), lambda i: (0, i))],
        out_shape=[jax.ShapeDtypeStruct((t, d), F32), jax.ShapeDtypeStruct((d, t), BF16)],
        compiler_params=_params("parallel"),
        name="mem_attn",
    )(h1, g_q, w_mq, mk, mv, w_mo, g_ffn)


def _extract_top(vals, key, count, exact):
    cur = vals
    rank = jnp.full(vals.shape, float(count), F32)
    tops = []
    big = jnp.float32(2**30)
    for r in range(count):
        m = jnp.max(cur, axis=0, keepdims=True)
        hit = cur == m
        if exact:
            first = jnp.min(jnp.where(hit, key, big), axis=0, keepdims=True)
            hit = key == first
        rank = jnp.where(hit, float(r), rank)
        cur = jnp.where(hit, -jnp.inf, cur)
        tops.append(m)
    return tops, rank


def _peer_route_kernel(w_ref, n3t_ref, keys_ref, pk2_ref, pk1_ref, qp_sc):
    qp_sc[...] = jnp.dot(w_ref[...], n3t_ref[...], preferred_element_type=F32)
    qd = 2 * PEER_HALF

    def head(hd, carry):
        r0 = pl.multiple_of(hd * qd, qd)
        s1 = jnp.dot(keys_ref[0], qp_sc[pl.ds(r0, PEER_HALF), :].astype(BF16), preferred_element_type=F32)
        s2 = jnp.dot(keys_ref[1], qp_sc[pl.ds(r0 + PEER_HALF, PEER_HALF), :].astype(BF16),
                     preferred_element_type=F32)

        def store(tables):
            rank2, e2, cnt, e1 = tables
            pk2_ref[hd, 0] = rank2.astype(BF16)
            pk2_ref[hd, 1] = e2.astype(BF16)
            pk1_ref[hd, 0] = cnt
            pk1_ref[hd, 1] = e1

        tables, removed = _route_tables(s1, s2, exact=False)
        store(tables)
        ties = jnp.max(jnp.abs(removed - float(3 * PEER_TOPK)))

        @pl.when(ties > 0.0)
        def _():
            store(_route_tables(s1, s2, exact=True)[0])

        return carry

    lax.fori_loop(0, PEER_HEADS, head, 0)


def _route_tables(s1, s2, exact):
    K = PEER_TOPK
    tm = s1.shape[1]
    kiota = lax.broadcasted_iota(jnp.int32, s1.shape, 0).astype(F32)
    a, rank1 = _extract_top(s1, kiota, K, exact)
    b, rank2 = _extract_top(s2, kiota, K, exact)

    bmat = jnp.concatenate(b, axis=0)
    jio = lax.broadcasted_iota(jnp.int32, (K, tm), 0).astype(F32)
    blocks, keys = [], []
    half = K // 2
    for i in range(half):
        blocks.append(jnp.where(jio < float(K // (i + 1)), a[i] + bmat, -jnp.inf))
        keys.append(jio + float(i * K))
    blocks.append(jnp.concatenate([a[i] + b[0] for i in range(half, K)], axis=0))
    keys.append((lax.broadcasted_iota(jnp.int32, (K - half, tm), 0).astype(F32) + float(half)) * float(K))
    cand = jnp.concatenate(blocks, axis=0)
    ckey = jnp.concatenate(keys, axis=0)
    _, crank = _extract_top(cand, ckey, K, exact)
    sel = crank < float(K)
    z = jnp.sum(jnp.where(sel, jnp.exp(cand - (a[0] + b[0])), 0.0), axis=0, keepdims=True)
    self32 = sel.astype(F32)
    counts = [jnp.sum(self32[i * K:(i + 1) * K], axis=0, keepdims=True) for i in range(half)]
    counts += [self32[half * K + i:half * K + i + 1] for i in range(K - half)]

    cnt = jnp.zeros(s1.shape, F32)
    for i in range(K):
        cnt = jnp.where(rank1 == float(i), counts[i], cnt)
    removed = (jnp.sum((rank1 < float(K)).astype(F32), axis=0, keepdims=True)
               + jnp.sum((rank2 < float(K)).astype(F32), axis=0, keepdims=True)
               + jnp.sum(self32, axis=0, keepdims=True))
    return (rank2, jnp.exp(s2 - b[0]), cnt, jnp.exp(s1 - a[0]) / z), removed


def _peer_route(w_pq_t, n3t, keys):
    d, t = n3t.shape
    tm = _pick(t, (512, 256, 128))
    qw = PEER_HEADS * 2 * PEER_HALF
    spec = pl.BlockSpec((PEER_HEADS, 2, PEER_NKEYS, tm), lambda i: (0, 0, 0, i))
    return pl.pallas_call(
        _peer_route_kernel,
        grid=(t // tm,),
        in_specs=[
            pl.BlockSpec((qw, d), lambda i: (0, 0), pipeline_mode=pl.Buffered(1)),
            pl.BlockSpec((d, tm), lambda i: (0, i)),
            pl.BlockSpec((2, PEER_NKEYS, PEER_HALF), lambda i: (0, 0, 0)),
        ],
        out_specs=[spec, spec],
        out_shape=[jax.ShapeDtypeStruct((PEER_HEADS, 2, PEER_NKEYS, t), BF16),
                   jax.ShapeDtypeStruct((PEER_HEADS, 2, PEER_NKEYS, t), F32)],
        scratch_shapes=[pltpu.VMEM((qw, tm), F32)],
        compiler_params=_params("parallel"),
        name="peer_route",
    )(w_pq_t, n3t, keys)


def _gelu_tanh(x):
    inner = x * (0.7978845608028654 + (0.7978845608028654 * 0.044715) * (x * x))
    half = 0.5 * x
    return half + half * jnp.tanh(inner)


def _peer_kernel(n3t_ref, pk2_ref, pk1_ref, u_ref, vt_ref, o_ref, act_sc, *, ec, n_chunks):
    e = pl.program_id(1)
    cur = e % 2

    @pl.when(e == 0)
    def _():
        o_ref[...] = jnp.zeros_like(o_ref)
        act_sc[1] = jnp.zeros(act_sc.shape[1:], act_sc.dtype)

    h = jnp.dot(u_ref[...], n3t_ref[...], preferred_element_type=F32)
    o_ref[...] += jnp.dot(vt_ref[0], act_sc[1 - cur], preferred_element_type=F32)
    k1_base = jnp.minimum(e, n_chunks - 1) * (ec // PEER_NKEYS)
    for kk in range(ec // PEER_NKEYS):
        rows = slice(kk * PEER_NKEYS, (kk + 1) * PEER_NKEYS)
        gate = jnp.zeros((PEER_NKEYS, h.shape[1]), BF16)
        for hd in range(PEER_HEADS):
            cnt = pk1_ref[hd, 0, pl.ds(k1_base + kk, 1), :].astype(BF16)
            e1 = pk1_ref[hd, 1, pl.ds(k1_base + kk, 1), :].astype(BF16)
            gate = gate + jnp.where(pk2_ref[hd, 0] < cnt, pk2_ref[hd, 1], 0.0) * e1
        act_sc[cur, rows, :] = (_gelu_tanh(h[rows]) * gate.astype(F32)).astype(BF16)


def _peer(n3t, pk2, pk1, u, vt):
    d, t = n3t.shape
    n_chunks, _, ec = vt.shape
    tm = _pick(t, (512, 256, 128))
    once = pl.Buffered(1)
    return pl.pallas_call(
        functools.partial(_peer_kernel, ec=ec, n_chunks=n_chunks),
        grid=(t // tm, n_chunks + 1),
        in_specs=[
            pl.BlockSpec((d, tm), lambda i, e: (0, i), pipeline_mode=once),
            pl.BlockSpec((PEER_HEADS, 2, PEER_NKEYS, tm), lambda i, e: (0, 0, 0, i), pipeline_mode=once),
            pl.BlockSpec((PEER_HEADS, 2, PEER_NKEYS, tm), lambda i, e: (0, 0, 0, i), pipeline_mode=once),
            pl.BlockSpec((ec, d), lambda i, e: (jnp.minimum(e, n_chunks - 1), 0)),
            pl.BlockSpec((1, d, ec), lambda i, e: (jnp.maximum(e - 1, 0), 0, 0)),
        ],
        out_specs=pl.BlockSpec((d, tm), lambda i, e: (0, i)),
        out_shape=jax.ShapeDtypeStruct((d, t), F32),
        scratch_shapes=[pltpu.VMEM((2, ec, tm), BF16)],
        compiler_params=_params("parallel", "arbitrary"),
        name="peer_experts",
    )(n3t, pk2, pk1, u, vt)


def _final_kernel(h_ref, pt_ref, g_ref, o_ref):
    x = h_ref[...] + pt_ref[...].T
    y = x * lax.rsqrt(jnp.mean(x * x, axis=-1, keepdims=True) + EPS)
    o_ref[...] = y * g_ref[...]


def _final(h2, peer_t, g):
    t, d = h2.shape
    tm = _pick(t, (256, 128))
    return pl.pallas_call(
        _final_kernel,
        grid=(t // tm,),
        in_specs=[pl.BlockSpec((tm, d), lambda i: (i, 0)), pl.BlockSpec((d, tm), lambda i: (0, i)),
                  pl.BlockSpec((1, d), lambda i: (0, 0))],
        out_specs=pl.BlockSpec((tm, d), lambda i: (i, 0)),
        out_shape=jax.ShapeDtypeStruct((t, d), F32),
        compiler_params=_params("parallel"),
        name="final_norm",
    )(h2, peer_t, g)


def _layer(x, w, mem_k, mem_v, batch, tokens, n_mem, past):
    n = _rmsnorm(x, w["norm_mix"])
    (q16,) = _mm(n, w["w_fox"], (BF16,), COL_FQ, FOX_WIDTH, scale=FOX_QSCALE, w_rows=True)
    fk, k16 = _mm(n, w["w_fox"], (F32, BF16), COL_FK, FOX_WIDTH, w_rows=True)
    fv, v16 = _mm(n, w["w_fox"], (F32, BF16), COL_FV, FOX_WIDTH, w_rows=True)
    z32, z16 = _mm(n, w["w_gla"], (F32, BF16), w_rows=True)
    small = _small_proj(n, w["w_small"], w["b_fgate"])
    logf = small[:, :FOX_HEADS]

    if past is None:
        lb = _pick(tokens, (512, 256, 128))
        c = _cumsum_t(small.reshape(batch, tokens, LANES), lb)
        tq = _pick(tokens, (512, 256, 128))
        c4 = c.reshape(batch * LANES, tokens // tq, 1, tq)
        fo = _fox_prompt(q16, k16, v16, c4, batch, tokens)
        go, gla_state = _gla(z32, z16, small, w["wa_pad"], w["b_gla_a"], w["norm_gla_out"], None, batch, tokens)
    else:
        ck, cv, clogf, s0 = past
        plen = ck.shape[1]
        lf = jnp.concatenate([
            jnp.pad(clogf.astype(F32), ((0, 0), (0, 0), (0, LANES - FOX_HEADS))),
            small.reshape(batch, tokens, LANES),
            jnp.zeros((batch, LANES - tokens, LANES), F32)], axis=1)
        c = _cumsum_t(lf, LANES)
        c3 = c.reshape(batch * LANES, 1, plen + LANES)
        fo = _fox_sample(q16, k16, v16, ck.reshape(batch, plen * FOX_HEADS, FOX_HEAD_DIM),
                         cv.reshape(batch, plen * FOX_HEADS, FOX_HEAD_DIM), c3, batch, tokens, plen)
        go, gla_state = _gla(z32, z16, small, w["wa_pad"], w["b_gla_a"], w["norm_gla_out"], s0.astype(F32),
                             batch, tokens)

    h1 = _outproj(fo, go, w["w_out"], x)
    h2, n3t = _mem_block(h1, w["norm_mem_q"], w["w_mq"], mem_k, mem_v, w["w_mo"], w["norm_ffn"], tokens, n_mem)
    pk2, pk1 = _peer_route(w["w_pq_t"], n3t, w["sub_keys"])
    peer_t = _peer(n3t, pk2, pk1, w["expert_u"], w["expert_v_t"])
    return h2, peer_t, fk, fv, logf, gla_state


def kernel(x_prompt, x_sample, cache_fox_k, cache_fox_v, cache_fox_logf, state_gla, cache_mem_k, cache_mem_v, mem_prompt, norm_mix, w_in, b_fgate, w_gla_a2, b_gla_a, norm_gla_out, w_out, norm_mem_q, norm_mem_kv, w_mq, w_mk, w_mv, w_mo, norm_ffn, w_pq, sub_keys, expert_u, expert_v, norm_final):
    depth = w_in.shape[0]
    assert depth == 1, "one trunk layer"
    bp, sp, d = x_prompt.shape
    bs, ss, _ = x_sample.shape
    n_mem = mem_prompt.shape[1]
    l = 0

    wt = jnp.swapaxes(w_in, 1, 2)[l]
    o_ff = 3 * FOX_WIDTH
    o_g = o_ff + FOX_HEADS
    o_ga = o_g + GLA_PROJ_WIDTH
    w = {
        "norm_mix": norm_mix[l],
        "w_fox": wt[:o_ff].astype(BF16),
        "w_gla": wt[o_g:o_ga].astype(BF16),
        "w_small": jnp.concatenate([wt[o_ff:o_g], wt[o_ga:],
                                    jnp.zeros((LANES - FOX_HEADS - GLA_GATE_RANK, d), F32)], axis=0).astype(BF16),
        "b_fgate": jnp.pad(b_fgate[l], (0, LANES - FOX_HEADS)).reshape(1, LANES),
        "wa_pad": jnp.pad(w_gla_a2[l], ((FOX_HEADS, LANES - FOX_HEADS - GLA_GATE_RANK), (0, 0))).astype(BF16),
        "b_gla_a": b_gla_a[l].reshape(1, GLA_KEY_WIDTH),
        "norm_gla_out": norm_gla_out[l].reshape(1, GLA_DV),
        "w_out": w_out[l].astype(BF16),
        "norm_mem_q": norm_mem_q[l].reshape(1, d),
        "w_mq": w_mq[l].astype(BF16),
        "w_mo": w_mo[l].astype(BF16),
        "norm_ffn": norm_ffn[l].reshape(1, d),
        "w_pq_t": w_pq[l].T.astype(BF16),
        "sub_keys": sub_keys[l].astype(BF16),
        "expert_u": expert_u[l].astype(BF16),
        "expert_v_t": expert_v[l].reshape(-1, PEER_EXPERT_CHUNK, d).transpose(0, 2, 1).astype(BF16),
    }

    m = _rmsnorm(mem_prompt.reshape(bp * n_mem, d), norm_mem_kv[l])
    (mk,) = _mm(m, w_mk[l].astype(BF16), (F32,))
    (mv,) = _mm(m, w_mv[l].astype(BF16), (F32,))

    hp, pp, fkp, fvp, lfp, gsp = _layer(x_prompt.reshape(bp * sp, d), w, mk, mv, bp, sp, n_mem, None)
    past = (cache_fox_k[l], cache_fox_v[l], cache_fox_logf[l], state_gla[l])
    hs, ps, fks, fvs, lfs, gss = _layer(x_sample.reshape(bs * ss, d), w,
                                        cache_mem_k[l].reshape(bs * n_mem, MEM_WIDTH),
                                        cache_mem_v[l].reshape(bs * n_mem, MEM_WIDTH), bs, ss, n_mem, past)

    g_fin = norm_final.reshape(1, d)
    y_prompt = _final(hp, pp, g_fin).reshape(bp, sp, d)
    y_sample = _final(hs, ps, g_fin).reshape(bs, ss, d)
    hshape_p = (1, bp, sp, FOX_HEADS, FOX_HEAD_DIM)
    hshape_s = (1, bs, ss, FOX_HEADS, FOX_HEAD_DIM)
    return (y_prompt, y_sample,
            fkp.reshape(hshape_p), fvp.reshape(hshape_p), lfp.reshape(1, bp, sp, FOX_HEADS), gsp[None],
            mk.reshape(1, bp, n_mem, MEM_HEADS, MEM_HEAD_DIM), mv.reshape(1, bp, n_mem, MEM_HEADS, MEM_HEAD_DIM),
            fks.reshape(hshape_s), fvs.reshape(hshape_s), lfs.reshape(1, bs, ss, FOX_HEADS), gss[None])
```

```python
import functools

import jax
import jax.numpy as jnp
from jax import lax
from jax.experimental import pallas as pl
from jax.experimental.pallas import tpu as pltpu

F32 = jnp.float32
BF16 = jnp.bfloat16

EPS = 1e-6
NEG_INF = -1e30

FOX_HEADS = 16
FOX_HEAD_DIM = 128
FOX_WIDTH = FOX_HEADS * FOX_HEAD_DIM
FOX_HEADS_PER_STEP = 8
LOG2E = 1.4426950408889634
FOX_QSCALE = FOX_HEAD_DIM ** -0.5 * LOG2E
GLA_HEADS = 4
GLA_DK = 256
GLA_DV = 512
GLA_KEY_WIDTH = GLA_HEADS * GLA_DK
GLA_WIDTH = GLA_HEADS * GLA_DV
GLA_GATE_RANK = 16
GLA_TAU = 16.0
GLA_CHUNK = 128
GLA_SUB = 32
GLA_HEADS_PER_STEP = 4
MEM_HEADS = 4
MEM_HEAD_DIM = 128
MEM_WIDTH = MEM_HEADS * MEM_HEAD_DIM
PEER_HEADS = 8
PEER_NKEYS = 128
PEER_HALF = 128
PEER_TOPK = 16
PEER_EXPERT_CHUNK = 512
LANES = 128
GLA_PROJ_WIDTH = 2 * GLA_KEY_WIDTH + 2 * GLA_WIDTH

COL_FQ, COL_FK, COL_FV = 0, FOX_WIDTH, 2 * FOX_WIDTH
COL_GQ = 0
COL_GK = COL_GQ + GLA_KEY_WIDTH
COL_GV = COL_GK + GLA_KEY_WIDTH
COL_GG = COL_GV + GLA_WIDTH

VMEM_LIMIT_BYTES = 52 * 2**20


def _pick(n, prefs):
    for p in prefs:
        if n % p == 0:
            return p
    raise ValueError(f"no tile in {prefs} divides {n}")


def _params(*sem):
    return pltpu.CompilerParams(dimension_semantics=sem, vmem_limit_bytes=VMEM_LIMIT_BYTES)


def _log_sigmoid(x):
    return -(jnp.maximum(-x, 0.0) + jnp.log1p(jnp.exp(-jnp.abs(x))))


def _split3(x):
    hi = x.astype(BF16)
    r1 = x - hi.astype(F32)
    mid = r1.astype(BF16)
    lo = (r1 - mid.astype(F32)).astype(BF16)
    return hi, mid, lo


def _rmsnorm_kernel(x_ref, g_ref, o_ref):
    x = x_ref[...]
    y = x * lax.rsqrt(jnp.mean(x * x, axis=-1, keepdims=True) + EPS)
    o_ref[...] = (y * g_ref[...]).astype(o_ref.dtype)


def _rmsnorm(x, g):
    t, d = x.shape
    tm = _pick(t, (512, 256, 128))
    return pl.pallas_call(
        _rmsnorm_kernel,
        grid=(t // tm,),
        in_specs=[pl.BlockSpec((tm, d), lambda i: (i, 0)), pl.BlockSpec((1, d), lambda i: (0, 0))],
        out_specs=pl.BlockSpec((tm, d), lambda i: (i, 0)),
        out_shape=jax.ShapeDtypeStruct((t, d), BF16),
        compiler_params=_params("parallel"),
        name="rmsnorm",
    )(x, g.reshape(1, d))


def _mm_kernel(a_ref, w_ref, *o_refs, scale, w_rows):
    if w_rows:
        r = lax.dot_general(a_ref[...], w_ref[...], (((1,), (1,)), ((), ())), preferred_element_type=F32)
    else:
        r = jnp.dot(a_ref[...], w_ref[...], preferred_element_type=F32)
    if scale is not None:
        r = r * scale
    for o_ref in o_refs:
        o_ref[...] = r.astype(o_ref.dtype)


def _mm(a, w, out_dtypes, col0=0, ncols=None, scale=None, w_rows=False):
    t, k = a.shape
    n = w.shape[0 if w_rows else 1] - col0 if ncols is None else ncols
    tm = _pick(t, (512, 256, 128))
    tn = _pick(n, (1024, 512, 256, 128))
    assert col0 % tn == 0
    j0 = col0 // tn
    if w_rows:
        w_spec = pl.BlockSpec((tn, k), lambda j, i: (j0 + j, 0))
    else:
        w_spec = pl.BlockSpec((k, tn), lambda j, i: (0, j0 + j))
    outs = pl.pallas_call(
        functools.partial(_mm_kernel, scale=scale, w_rows=w_rows),
        grid=(n // tn, t // tm),
        in_specs=[pl.BlockSpec((tm, k), lambda j, i: (i, 0)), w_spec],
        out_specs=[pl.BlockSpec((tm, tn), lambda j, i: (i, j)) for _ in out_dtypes],
        out_shape=[jax.ShapeDtypeStruct((t, n), dt) for dt in out_dtypes],
        compiler_params=_params("parallel", "parallel"),
        name="proj",
    )(a, w)
    return outs


def _norm_mm_kernel(x_ref, g_ref, w_ref, n_ref, o_ref, *, scale):
    x = x_ref[...]
    n = (x * lax.rsqrt(jnp.mean(x * x, axis=-1, keepdims=True) + EPS) * g_ref[...]).astype(BF16)
    n_ref[...] = n
    r = lax.dot_general(n, w_ref[...], (((1,), (1,)), ((), ())), preferred_element_type=F32)
    o_ref[...] = (r * scale).astype(o_ref.dtype)


def _norm_mm(x, g, w_t, ncols, scale):
    t, k = x.shape
    tm = _pick(t, (512, 256, 128))
    return pl.pallas_call(
        functools.partial(_norm_mm_kernel, scale=scale),
        grid=(t // tm,),
        in_specs=[pl.BlockSpec((tm, k), lambda i: (i, 0)), pl.BlockSpec((1, k), lambda i: (0, 0)),
                  pl.BlockSpec((ncols, k), lambda i: (0, 0), pipeline_mode=pl.Buffered(1))],
        out_specs=[pl.BlockSpec((tm, k), lambda i: (i, 0)), pl.BlockSpec((tm, ncols), lambda i: (i, 0))],
        out_shape=[jax.ShapeDtypeStruct((t, k), BF16), jax.ShapeDtypeStruct((t, ncols), BF16)],
        compiler_params=_params("parallel"),
        name="norm_proj",
    )(x, g.reshape(1, k), w_t)


def _small_kernel(n_ref, w_ref, b_ref, o_ref):
    z = lax.dot_general(n_ref[...], w_ref[...], (((1,), (1,)), ((), ())), preferred_element_type=F32)
    col = lax.broadcasted_iota(jnp.int32, z.shape, 1)
    o_ref[...] = jnp.where(col < FOX_HEADS, _log_sigmoid(z + b_ref[...]), z)


def _small_proj(n, w_small_t, b_fgate_pad):
    t, k = n.shape
    tm = _pick(t, (512, 256, 128))
    return pl.pallas_call(
        _small_kernel,
        grid=(t // tm,),
        in_specs=[pl.BlockSpec((tm, k), lambda i: (i, 0)), pl.BlockSpec((LANES, k), lambda i: (0, 0)),
                  pl.BlockSpec((1, LANES), lambda i: (0, 0))],
        out_specs=pl.BlockSpec((tm, LANES), lambda i: (i, 0)),
        out_shape=jax.ShapeDtypeStruct((t, LANES), F32),
        compiler_params=_params("parallel"),
        name="small_proj",
    )(n, w_small_t, b_fgate_pad)


def _cumsum_kernel(x_ref, o_ref, carry_ref, *, lb):
    @pl.when(pl.program_id(1) == 0)
    def _():
        carry_ref[...] = jnp.zeros_like(carry_ref)

    xt = x_ref[0].T
    r = lax.broadcasted_iota(jnp.int32, (lb, lb), 0)
    c = lax.broadcasted_iota(jnp.int32, (lb, lb), 1)
    tri = (r <= c).astype(BF16)
    acc = carry_ref[...]
    for part in _split3(xt):
        acc = acc + jnp.dot(part, tri, preferred_element_type=F32)
    o_ref[0] = acc
    carry_ref[...] = jnp.broadcast_to(acc[:, lb - 1:lb], carry_ref.shape)


def _cumsum_t(x, lb):
    b, l, _ = x.shape
    return pl.pallas_call(
        functools.partial(_cumsum_kernel, lb=lb),
        grid=(b, l // lb),
        in_specs=[pl.BlockSpec((1, lb, LANES), lambda i, j: (i, j, 0))],
        out_specs=pl.BlockSpec((1, LANES, lb), lambda i, j: (i, 0, j)),
        out_shape=jax.ShapeDtypeStruct((b, LANES, l), F32),
        scratch_shapes=[pltpu.VMEM((LANES, lb), F32)],
        compiler_params=_params("parallel", "arbitrary"),
        name="cumsum_logf",
    )(x)


def _fox_prompt_kernel(q_ref, k_ref, v_ref, c_ref, o_ref, m_sc, acc_sc, *, tq, hb):
    qi = pl.program_id(2)
    m_sc[...] = jnp.full_like(m_sc, NEG_INF)
    acc_sc[...] = jnp.zeros_like(acc_sc)

    def chunk(kc, masked):
        ks = pl.multiple_of(kc * tq, tq)
        ones = jnp.ones((tq, LANES), BF16)
        for hh in range(hb):
            cs = slice(hh * FOX_HEAD_DIM, (hh + 1) * FOX_HEAD_DIM)
            s = lax.dot_general(q_ref[:, cs], k_ref[pl.ds(ks, tq), cs], (((1,), (1,)), ((), ())),
                                preferred_element_type=F32) - c_ref[hh, kc] * LOG2E
            if masked:
                r = lax.broadcasted_iota(jnp.int32, s.shape, 0)
                c = lax.broadcasted_iota(jnp.int32, s.shape, 1)
                s = jnp.where(r >= c, s, NEG_INF)
            m_prev = m_sc[hh]
            m_new = jnp.maximum(m_prev, jnp.max(s, axis=1, keepdims=True))
            alpha = jnp.exp2(m_prev - m_new)
            p = jnp.exp2(s - jnp.concatenate([m_new] * (tq // LANES), axis=1))
            v_ext = jnp.concatenate([v_ref[pl.ds(ks, tq), cs], ones], axis=1)
            pv = jnp.dot(p.astype(BF16), v_ext, preferred_element_type=F32)
            acc_sc[hh] = jnp.concatenate([alpha, alpha], axis=1) * acc_sc[hh] + pv
            m_sc[hh] = m_new

    def body(kc, carry):
        chunk(kc, False)
        return carry

    lax.fori_loop(0, qi, body, 0)
    chunk(qi, True)
    for hh in range(hb):
        acc = acc_sc[hh]
        o_ref[:, hh * FOX_HEAD_DIM:(hh + 1) * FOX_HEAD_DIM] = (
            acc[:, :FOX_HEAD_DIM] / acc[:, FOX_HEAD_DIM:]).astype(o_ref.dtype)


def _fox_prompt(q16, k16, v16, c4, batch, seq):
    tq = _pick(seq, (512, 256, 128))
    nq = seq // tq
    hb = FOX_HEADS_PER_STEP
    wid = hb * FOX_HEAD_DIM
    return pl.pallas_call(
        functools.partial(_fox_prompt_kernel, tq=tq, hb=hb),
        grid=(batch, FOX_HEADS // hb, nq),
        in_specs=[
            pl.BlockSpec((tq, wid), lambda b, h, i: (b * nq + i, h)),
            pl.BlockSpec((seq, wid), lambda b, h, i: (b, h)),
            pl.BlockSpec((seq, wid), lambda b, h, i: (b, h)),
            pl.BlockSpec((hb, nq, 1, tq), lambda b, h, i: (b * (LANES // hb) + h, 0, 0, 0)),
        ],
        out_specs=pl.BlockSpec((tq, wid), lambda b, h, i: (b * nq + i, h)),
        out_shape=jax.ShapeDtypeStruct((batch * seq, FOX_WIDTH), BF16),
        scratch_shapes=[pltpu.VMEM((hb, tq, LANES), F32), pltpu.VMEM((hb, tq, 2 * FOX_HEAD_DIM), F32)],
        compiler_params=_params("parallel", "parallel", "arbitrary"),
        name="fox_prompt",
    )(q16, k16, v16, c4)


def _fox_sample_kernel(q_ref, kn_ref, vn_ref, ck_ref, cv_ref, c_ref, o_ref, *, past, new):
    pad = jnp.zeros((LANES - new, FOX_HEAD_DIM), BF16)
    dn = (((1,), (1,)), ((), ()))
    r = lax.broadcasted_iota(jnp.int32, (new, LANES), 0)
    cc = lax.broadcasted_iota(jnp.int32, (new, LANES), 1)
    for hd in range(FOX_HEADS):
        cs = slice(hd * FOX_HEAD_DIM, (hd + 1) * FOX_HEAD_DIM)
        q = q_ref[:, cs]
        kn = jnp.concatenate([kn_ref[:, cs], pad], axis=0)
        vn = jnp.concatenate([vn_ref[:, cs], pad], axis=0)
        c = c_ref[hd] * LOG2E
        ck = ck_ref[0, pl.ds(hd, past, stride=FOX_HEADS), :].astype(BF16)
        cv = cv_ref[0, pl.ds(hd, past, stride=FOX_HEADS), :].astype(BF16)
        s1 = lax.dot_general(q, ck, dn, preferred_element_type=F32) - c[:, :past]
        s2 = lax.dot_general(q, kn, dn, preferred_element_type=F32) - c[:, past:]
        s2 = jnp.where(r >= cc, s2, NEG_INF)
        m = jnp.maximum(jnp.max(s1, axis=1, keepdims=True), jnp.max(s2, axis=1, keepdims=True))
        p1 = jnp.exp2(s1 - m)
        p2 = jnp.exp2(s2 - m)
        l = jnp.sum(p1, axis=1, keepdims=True) + jnp.sum(p2, axis=1, keepdims=True)
        o = (jnp.dot(p1.astype(BF16), cv, preferred_element_type=F32)
             + jnp.dot(p2.astype(BF16), vn, preferred_element_type=F32))
        o_ref[:, cs] = (o / l).astype(o_ref.dtype)


def _fox_sample(q16, k16, v16, ck, cv, c3, batch, new, past):
    cache = pl.BlockSpec((1, past * FOX_HEADS, FOX_HEAD_DIM), lambda b: (b, 0, 0))
    rows = pl.BlockSpec((new, FOX_WIDTH), lambda b: (b, 0))
    return pl.pallas_call(
        functools.partial(_fox_sample_kernel, past=past, new=new),
        grid=(batch,),
        in_specs=[rows, rows, rows, cache, cache,
                  pl.BlockSpec((FOX_HEADS, 1, past + LANES), lambda b: (b * (LANES // FOX_HEADS), 0, 0))],
        out_specs=rows,
        out_shape=jax.ShapeDtypeStruct((batch * new, FOX_WIDTH), BF16),
        compiler_params=_params("parallel"),
        name="fox_sample",
    )(q16, k16, v16, ck, cv, c3)


def _gla_kernel(q_ref, k_ref, v_ref, gg_ref, sm_ref, wa_ref, ba_ref, g_ref, s0_ref, go_ref, sout_ref, s_sc,
                *, valid, has_init, hb):
    ci = pl.program_id(2)

    @pl.when(ci == 0)
    def _():
        if has_init:
            s_sc[...] = s0_ref[0]
        else:
            s_sc[...] = jnp.zeros_like(s_sc)

    for hh in range(hb):
        ks = slice(hh * GLA_DK, (hh + 1) * GLA_DK)
        vs = slice(hh * GLA_DV, (hh + 1) * GLA_DV)
        _gla_chunk(q_ref.at[:, ks], k_ref.at[:, ks], v_ref.at[:, vs], gg_ref.at[:, vs], sm_ref,
                   wa_ref.at[:, ks], ba_ref.at[:, ks], g_ref, s_sc.at[hh], go_ref.at[:, vs], valid)

    @pl.when(ci == pl.num_programs(2) - 1)
    def _():
        sout_ref[0] = s_sc[...]


def _gla_chunk(q_ref, k_ref, v_ref, gg_ref, sm_ref, wa_ref, ba_ref, g_ref, s_ref, go_ref, valid):
    C = GLA_CHUNK

    def rows(ref, dtype):
        x = ref[...].astype(dtype)
        if valid < C:
            x = jnp.concatenate([x, jnp.zeros((C - valid, x.shape[1]), dtype)], axis=0)
        return x

    row = lax.broadcasted_iota(jnp.int32, (C, C), 0)
    col = lax.broadcasted_iota(jnp.int32, (C, C), 1)

    a = jnp.dot(rows(sm_ref, BF16), wa_ref[...], preferred_element_type=F32) + ba_ref[...]
    lg = _log_sigmoid(a) * (1.0 / GLA_TAU)
    if valid < C:
        lg = jnp.where(lax.broadcasted_iota(jnp.int32, lg.shape, 0) < valid, lg, 0.0)
    low = (row >= col).astype(BF16)
    bc = jnp.zeros_like(lg)
    for part in _split3(lg):
        bc = bc + jnp.dot(low, part, preferred_element_type=F32)

    q = rows(q_ref, F32) * (GLA_DK ** -0.5)
    k = rows(k_ref, F32)
    v = rows(v_ref, BF16)
    s = s_ref[...]

    o_state = jnp.dot((q * jnp.exp(bc)).astype(BF16), s.astype(BF16), preferred_element_type=F32)
    krow = lax.broadcasted_iota(jnp.int32, (C, GLA_DK), 0)
    sub_row = lax.broadcasted_iota(jnp.int32, (GLA_SUB, C), 0)
    sub_col = lax.broadcasted_iota(jnp.int32, (GLA_SUB, C), 1)
    outs = []
    for i in range(C // GLA_SUB):
        r0, r1 = i * GLA_SUB, (i + 1) * GLA_SUB
        if valid <= r0:
            outs.append(o_state[r0:r1])
            continue
        base = bc[r0 - 1:r0] if i > 0 else jnp.zeros((1, GLA_DK), F32)
        qt = (q[r0:r1] * jnp.exp(bc[r0:r1] - base)).astype(BF16)
        kt = jnp.where(krow < r1, k * jnp.exp(base - bc), 0.0).astype(BF16)
        att = lax.dot_general(qt, kt, (((1,), (1,)), ((), ())), preferred_element_type=F32)
        att = jnp.where(sub_col <= sub_row + r0, att, 0.0)
        outs.append(o_state[r0:r1] + jnp.dot(att.astype(BF16), v, preferred_element_type=F32))
    o = jnp.concatenate(outs, axis=0)

    b_end = bc[C - 1:C]
    kd = k * jnp.exp(b_end - bc)
    decay = jnp.broadcast_to(jnp.exp(b_end), (LANES, GLA_DK)).T
    decay = jnp.concatenate([decay] * (GLA_DV // LANES), axis=1)
    s_new = decay * s + jnp.dot(kd.T.astype(BF16), v, preferred_element_type=F32)
    s_ref[...] = s_new

    o = o[:valid]
    on = o * lax.rsqrt(jnp.mean(o * o, axis=-1, keepdims=True) + EPS) * g_ref[...]
    gg = gg_ref[...]
    go_ref[...] = (on * (gg * jax.nn.sigmoid(gg))).astype(go_ref.dtype)


def _gla(z32, z16, small, wa_pad, b_a, g_out, s0, batch, tokens):
    if tokens >= GLA_CHUNK:
        assert tokens % GLA_CHUNK == 0
        blk, nch = GLA_CHUNK, tokens // GLA_CHUNK
    else:
        blk, nch = tokens, 1
    hb = GLA_HEADS_PER_STEP
    kw, vw = hb * GLA_DK, hb * GLA_DV
    has_init = s0 is not None
    if s0 is None:
        s0 = jnp.zeros((1, hb, GLA_DK, GLA_DV), F32)
        s0_map = lambda b, h, c: (0, 0, 0, 0)
    else:
        s0_map = lambda b, h, c: (b, h, 0, 0)
    qb, kb = COL_GQ // kw, COL_GK // kw
    vb, gb = COL_GV // vw, COL_GG // vw
    go, s_out = pl.pallas_call(
        functools.partial(_gla_kernel, valid=blk, has_init=has_init, hb=hb),
        grid=(batch, GLA_HEADS // hb, nch),
        in_specs=[
            pl.BlockSpec((blk, kw), lambda b, h, c: (b * nch + c, qb + h)),
            pl.BlockSpec((blk, kw), lambda b, h, c: (b * nch + c, kb + h)),
            pl.BlockSpec((blk, vw), lambda b, h, c: (b * nch + c, vb + h)),
            pl.BlockSpec((blk, vw), lambda b, h, c: (b * nch + c, gb + h)),
            pl.BlockSpec((blk, LANES), lambda b, h, c: (b * nch + c, 0)),
            pl.BlockSpec((LANES, kw), lambda b, h, c: (0, h)),
            pl.BlockSpec((1, kw), lambda b, h, c: (0, h)),
            pl.BlockSpec((1, GLA_DV), lambda b, h, c: (0, 0)),
            pl.BlockSpec((1, hb, GLA_DK, GLA_DV), s0_map),
        ],
        out_specs=[
            pl.BlockSpec((blk, vw), lambda b, h, c: (b * nch + c, h)),
            pl.BlockSpec((1, hb, GLA_DK, GLA_DV), lambda b, h, c: (b, h, 0, 0)),
        ],
        out_shape=[jax.ShapeDtypeStruct((batch * tokens, GLA_WIDTH), BF16),
                   jax.ShapeDtypeStruct((batch, GLA_HEADS, GLA_DK, GLA_DV), F32)],
        scratch_shapes=[pltpu.VMEM((hb, GLA_DK, GLA_DV), F32)],
        compiler_params=_params("parallel", "parallel", "arbitrary"),
        name="gla",
    )(z32, z32, z16, z32, small, wa_pad, b_a, g_out, s0)
    return go, s_out


def _outproj_kernel(fo_ref, go_ref, wt_ref, wb_ref, x_ref, o_ref):
    o_ref[...] = (x_ref[...] + jnp.dot(fo_ref[...], wt_ref[...], preferred_element_type=F32)
                  + jnp.dot(go_ref[...], wb_ref[...], preferred_element_type=F32))


def _outproj(fo, go, w_out, x):
    t, d = x.shape
    half = fo.shape[1]
    tm = _pick(t, (512, 256, 128))
    tn = _pick(d, (1024, 512))
    return pl.pallas_call(
        _outproj_kernel,
        grid=(d // tn, t // tm),
        in_specs=[
            pl.BlockSpec((tm, half), lambda j, i: (i, 0)),
            pl.BlockSpec((tm, half), lambda j, i: (i, 0)),
            pl.BlockSpec((half, tn), lambda j, i: (0, j)),
            pl.BlockSpec((half, tn), lambda j, i: (1, j)),
            pl.BlockSpec((tm, tn), lambda j, i: (i, j)),
        ],
        out_specs=pl.BlockSpec((tm, tn), lambda j, i: (i, j)),
        out_shape=jax.ShapeDtypeStruct((t, d), F32),
        compiler_params=_params("parallel", "parallel"),
        name="outproj",
    )(fo, go, w_out, w_out, x)


def _mem_kernel(h_ref, gq_ref, wq_ref, mk_ref, mv_ref, wo_ref, gf_ref, h2_ref, n3t_ref, *, per_batch, n_mem):
    h = h_ref[...]
    tm = h.shape[0]
    n2 = (h * lax.rsqrt(jnp.mean(h * h, axis=-1, keepdims=True) + EPS) * gq_ref[...]).astype(BF16)
    q = jnp.dot(n2, wq_ref[...], preferred_element_type=F32)
    scale = MEM_HEAD_DIM ** -0.5
    rows_per = min(per_batch, tm)
    parts = []
    for b in range(tm // rows_per):
        heads = []
        for hd in range(MEM_HEADS):
            cs = slice(hd * MEM_HEAD_DIM, (hd + 1) * MEM_HEAD_DIM)
            qh = q[b * rows_per:(b + 1) * rows_per, cs].astype(BF16)
            kh = mk_ref[b * n_mem:(b + 1) * n_mem, cs].astype(BF16)
            vh = mv_ref[b * n_mem:(b + 1) * n_mem, cs].astype(BF16)
            s = lax.dot_general(qh, kh, (((1,), (1,)), ((), ())), preferred_element_type=F32) * scale
            s = s - jnp.max(s, axis=1, keepdims=True)
            p = jnp.exp(s)
            p = p / jnp.sum(p, axis=1, keepdims=True)
            heads.append(jnp.dot(p.astype(BF16), vh, preferred_element_type=F32))
        parts.append(jnp.concatenate(heads, axis=1))
    o = jnp.concatenate(parts, axis=0).astype(BF16)
    h2 = h + jnp.dot(o, wo_ref[...], preferred_element_type=F32)
    h2_ref[...] = h2
    n3 = h2 * lax.rsqrt(jnp.mean(h2 * h2, axis=-1, keepdims=True) + EPS) * gf_ref[...]
    n3t_ref[...] = n3.T.astype(BF16)


def _mem_block(h1, g_q, w_mq, mk, mv, w_mo, g_ffn, per_batch, n_mem):
    t, d = h1.shape
    tm = _pick(t, (256, 128)) if per_batch >= 256 else LANES
    nb = max(tm // per_batch, 1)
    tiles_per_batch = max(per_batch // tm, 1)
    return pl.pallas_call(
        functools.partial(_mem_kernel, per_batch=per_batch, n_mem=n_mem),
        grid=(t // tm,),
        in_specs=[
            pl.BlockSpec((tm, d), lambda i: (i, 0)),
            pl.BlockSpec((1, d), lambda i: (0, 0)),
            pl.BlockSpec((d, MEM_WIDTH), lambda i: (0, 0)),
            pl.BlockSpec((nb * n_mem, MEM_WIDTH), lambda i: (i // tiles_per_batch, 0)),
            pl.BlockSpec((nb * n_mem, MEM_WIDTH), lambda i: (i // tiles_per_batch, 0)),
            pl.BlockSpec((MEM_WIDTH, d), lambda i: (0, 0)),
            pl.BlockSpec((1, d), lambda i: (0, 0)),
        ],
        out_specs=[pl.BlockSpec((tm, d), lambda i: (i, 0)), pl.BlockSpec((d, tm), lambda i: (0, i))],
        out_shape=[jax.ShapeDtypeStruct((t, d), F32), jax.ShapeDtypeStruct((d, t), BF16)],
        compiler_params=_params("parallel"),
        name="mem_attn",
    )(h1, g_q, w_mq, mk, mv, w_mo, g_ffn)


def _extract_top(vals, key, count, exact):
    cur = vals
    rank = jnp.full(vals.shape, float(count), F32)
    tops = []
    big = jnp.float32(2**30)
    for r in range(count):
        m = jnp.max(cur, axis=0, keepdims=True)
        hit = cur == m
        if exact:
            first = jnp.min(jnp.where(hit, key, big), axis=0, keepdims=True)
            hit = key == first
        rank = jnp.where(hit, float(r), rank)
        cur = jnp.where(hit, -jnp.inf, cur)
        tops.append(m)
    return tops, rank


def _peer_route_kernel(w_ref, n3t_ref, keys_ref, pk2_ref, pk1_ref, qp_sc):
    qp_sc[...] = jnp.dot(w_ref[...], n3t_ref[...], preferred_element_type=F32)
    qd = 2 * PEER_HALF

    def head(hd, carry):
        r0 = pl.multiple_of(hd * qd, qd)
        s1 = jnp.dot(keys_ref[0], qp_sc[pl.ds(r0, PEER_HALF), :].astype(BF16), preferred_element_type=F32)
        s2 = jnp.dot(keys_ref[1], qp_sc[pl.ds(r0 + PEER_HALF, PEER_HALF), :].astype(BF16),
                     preferred_element_type=F32)

        def store(tables):
            rank2, e2, cnt, e1 = tables
            pk2_ref[hd, 0] = rank2.astype(BF16)
            pk2_ref[hd, 1] = e2.astype(BF16)
            pk1_ref[hd, 0] = cnt
            pk1_ref[hd, 1] = e1

        tables, removed = _route_tables(s1, s2, exact=False)
        store(tables)
        ties = jnp.max(jnp.abs(removed - float(3 * PEER_TOPK)))

        @pl.when(ties > 0.0)
        def _():
            store(_route_tables(s1, s2, exact=True)[0])

        return carry

    lax.fori_loop(0, PEER_HEADS, head, 0)


def _route_tables(s1, s2, exact):
    K = PEER_TOPK
    tm = s1.shape[1]
    kiota = lax.broadcasted_iota(jnp.int32, s1.shape, 0).astype(F32)
    a, rank1 = _extract_top(s1, kiota, K, exact)
    b, rank2 = _extract_top(s2, kiota, K, exact)

    half = K // 2
    bmat = jnp.concatenate(b, axis=0)
    jio = lax.broadcasted_iota(jnp.int32, (K, tm), 0).astype(F32)
    jio_h = lax.broadcasted_iota(jnp.int32, (half, tm), 0).astype(F32)
    blocks, keys, spans = [a[0] + bmat], [jio], [(0, K)]
    for i in range(1, half):
        blocks.append(jnp.where(jio_h < float(K // (i + 1)), a[i] + bmat[:half], -jnp.inf))
        keys.append(jio_h + float(i * K))
        spans.append((K + (i - 1) * half, K + i * half))
    blocks.append(jnp.concatenate([a[i] + b[0] for i in range(half, K)], axis=0))
    keys.append((jio_h + float(half)) * float(K))
    tail = K + (half - 1) * half
    spans += [(tail + i, tail + i + 1) for i in range(K - half)]
    cand = jnp.concatenate(blocks, axis=0)
    ckey = jnp.concatenate(keys, axis=0)
    _, crank = _extract_top(cand, ckey, K, exact)
    sel = crank < float(K)
    z = jnp.sum(jnp.where(sel, jnp.exp(cand - (a[0] + b[0])), 0.0), axis=0, keepdims=True)
    self32 = sel.astype(F32)
    counts = [jnp.sum(self32[lo:hi], axis=0, keepdims=True) for lo, hi in spans]

    cnt = jnp.zeros(s1.shape, F32)
    for i in range(K):
        cnt = jnp.where(rank1 == float(i), counts[i], cnt)
    removed = (jnp.sum((rank1 < float(K)).astype(F32), axis=0, keepdims=True)
               + jnp.sum((rank2 < float(K)).astype(F32), axis=0, keepdims=True)
               + jnp.sum(self32, axis=0, keepdims=True))
    return (rank2, jnp.exp(s2 - b[0]), cnt, jnp.exp(s1 - a[0]) / z), removed


def _peer_route(w_pq_t, n3t, keys):
    d, t = n3t.shape
    tm = _pick(t, (512, 256, 128))
    qw = PEER_HEADS * 2 * PEER_HALF
    spec = pl.BlockSpec((PEER_HEADS, 2, PEER_NKEYS, tm), lambda i: (0, 0, 0, i))
    return pl.pallas_call(
        _peer_route_kernel,
        grid=(t // tm,),
        in_specs=[
            pl.BlockSpec((qw, d), lambda i: (0, 0), pipeline_mode=pl.Buffered(1)),
            pl.BlockSpec((d, tm), lambda i: (0, i)),
            pl.BlockSpec((2, PEER_NKEYS, PEER_HALF), lambda i: (0, 0, 0)),
        ],
        out_specs=[spec, spec],
        out_shape=[jax.ShapeDtypeStruct((PEER_HEADS, 2, PEER_NKEYS, t), BF16),
                   jax.ShapeDtypeStruct((PEER_HEADS, 2, PEER_NKEYS, t), F32)],
        scratch_shapes=[pltpu.VMEM((qw, tm), F32)],
        compiler_params=_params("parallel"),
        name="peer_route",
    )(w_pq_t, n3t, keys)


def _gelu_tanh(x):
    inner = x * (0.7978845608028654 + (0.7978845608028654 * 0.044715) * (x * x))
    half = 0.5 * x
    return half + half * jnp.tanh(inner)


def _peer_kernel(n3t_ref, pk2_ref, pk1_ref, u_ref, vt_ref, o_ref, act_sc, *, ec, n_chunks):
    e = pl.program_id(1)
    cur = e % 2

    @pl.when(e == 0)
    def _():
        o_ref[...] = jnp.zeros_like(o_ref)
        act_sc[1] = jnp.zeros(act_sc.shape[1:], act_sc.dtype)

    h = jnp.dot(u_ref[...], n3t_ref[...], preferred_element_type=F32)
    o_ref[...] += jnp.dot(vt_ref[0], act_sc[1 - cur], preferred_element_type=F32)
    k1_base = jnp.minimum(e, n_chunks - 1) * (ec // PEER_NKEYS)
    for kk in range(ec // PEER_NKEYS):
        rows = slice(kk * PEER_NKEYS, (kk + 1) * PEER_NKEYS)
        gate = jnp.zeros((PEER_NKEYS, h.shape[1]), BF16)
        for hd in range(PEER_HEADS):
            cnt = pk1_ref[hd, 0, pl.ds(k1_base + kk, 1), :].astype(BF16)
            e1 = pk1_ref[hd, 1, pl.ds(k1_base + kk, 1), :].astype(BF16)
            gate = gate + jnp.where(pk2_ref[hd, 0] < cnt, pk2_ref[hd, 1], 0.0) * e1
        act_sc[cur, rows, :] = (_gelu_tanh(h[rows]) * gate.astype(F32)).astype(BF16)


def _peer(n3t, pk2, pk1, u, vt):
    d, t = n3t.shape
    n_chunks, _, ec = vt.shape
    tm = _pick(t, (512, 256, 128))
    once = pl.Buffered(1)
    return pl.pallas_call(
        functools.partial(_peer_kernel, ec=ec, n_chunks=n_chunks),
        grid=(t // tm, n_chunks + 1),
        in_specs=[
            pl.BlockSpec((d, tm), lambda i, e: (0, i), pipeline_mode=once),
            pl.BlockSpec((PEER_HEADS, 2, PEER_NKEYS, tm), lambda i, e: (0, 0, 0, i), pipeline_mode=once),
            pl.BlockSpec((PEER_HEADS, 2, PEER_NKEYS, tm), lambda i, e: (0, 0, 0, i), pipeline_mode=once),
            pl.BlockSpec((ec, d), lambda i, e: (jnp.minimum(e, n_chunks - 1), 0)),
            pl.BlockSpec((1, d, ec), lambda i, e: (jnp.maximum(e - 1, 0), 0, 0)),
        ],
        out_specs=pl.BlockSpec((d, tm), lambda i, e: (0, i)),
        out_shape=jax.ShapeDtypeStruct((d, t), F32),
        scratch_shapes=[pltpu.VMEM((2, ec, tm), BF16)],
        compiler_params=_params("parallel", "arbitrary"),
        name="peer_experts",
    )(n3t, pk2, pk1, u, vt)


def _final_kernel(h_ref, pt_ref, g_ref, o_ref):
    x = h_ref[...] + pt_ref[...].T
    y = x * lax.rsqrt(jnp.mean(x * x, axis=-1, keepdims=True) + EPS)
    o_ref[...] = y * g_ref[...]


def _final(h2, peer_t, g):
    t, d = h2.shape
    tm = _pick(t, (256, 128))
    return pl.pallas_call(
        _final_kernel,
        grid=(t // tm,),
        in_specs=[pl.BlockSpec((tm, d), lambda i: (i, 0)), pl.BlockSpec((d, tm), lambda i: (0, i)),
                  pl.BlockSpec((1, d), lambda i: (0, 0))],
        out_specs=pl.BlockSpec((tm, d), lambda i: (i, 0)),
        out_shape=jax.ShapeDtypeStruct((t, d), F32),
        compiler_params=_params("parallel"),
        name="final_norm",
    )(h2, peer_t, g)


def _layer(x, w, mem_k, mem_v, batch, tokens, n_mem, past):
    n, q16 = _norm_mm(x, w["norm_mix"], w["w_fox"], FOX_WIDTH, FOX_QSCALE)
    fk, k16 = _mm(n, w["w_fox"], (F32, BF16), COL_FK, FOX_WIDTH, w_rows=True)
    fv, v16 = _mm(n, w["w_fox"], (F32, BF16), COL_FV, FOX_WIDTH, w_rows=True)
    z32, z16 = _mm(n, w["w_gla"], (F32, BF16), w_rows=True)
    small = _small_proj(n, w["w_small"], w["b_fgate"])
    logf = small[:, :FOX_HEADS]

    if past is None:
        lb = _pick(tokens, (512, 256, 128))
        c = _cumsum_t(small.reshape(batch, tokens, LANES), lb)
        tq = _pick(tokens, (512, 256, 128))
        c4 = c.reshape(batch * LANES, tokens // tq, 1, tq)
        fo = _fox_prompt(q16, k16, v16, c4, batch, tokens)
        go, gla_state = _gla(z32, z16, small, w["wa_pad"], w["b_gla_a"], w["norm_gla_out"], None, batch, tokens)
    else:
        ck, cv, clogf, s0 = past
        plen = ck.shape[1]
        lf = jnp.concatenate([
            jnp.pad(clogf.astype(F32), ((0, 0), (0, 0), (0, LANES - FOX_HEADS))),
            small.reshape(batch, tokens, LANES),
            jnp.zeros((batch, LANES - tokens, LANES), F32)], axis=1)
        c = _cumsum_t(lf, LANES)
        c3 = c.reshape(batch * LANES, 1, plen + LANES)
        fo = _fox_sample(q16, k16, v16, ck.reshape(batch, plen * FOX_HEADS, FOX_HEAD_DIM),
                         cv.reshape(batch, plen * FOX_HEADS, FOX_HEAD_DIM), c3, batch, tokens, plen)
        go, gla_state = _gla(z32, z16, small, w["wa_pad"], w["b_gla_a"], w["norm_gla_out"], s0.astype(F32),
                             batch, tokens)

    h1 = _outproj(fo, go, w["w_out"], x)
    h2, n3t = _mem_block(h1, w["norm_mem_q"], w["w_mq"], mem_k, mem_v, w["w_mo"], w["norm_ffn"], tokens, n_mem)
    pk2, pk1 = _peer_route(w["w_pq_t"], n3t, w["sub_keys"])
    peer_t = _peer(n3t, pk2, pk1, w["expert_u"], w["expert_v_t"])
    return h2, peer_t, fk, fv, logf, gla_state


def kernel(x_prompt, x_sample, cache_fox_k, cache_fox_v, cache_fox_logf, state_gla, cache_mem_k, cache_mem_v, mem_prompt, norm_mix, w_in, b_fgate, w_gla_a2, b_gla_a, norm_gla_out, w_out, norm_mem_q, norm_mem_kv, w_mq, w_mk, w_mv, w_mo, norm_ffn, w_pq, sub_keys, expert_u, expert_v, norm_final):
    depth = w_in.shape[0]
    assert depth == 1, "one trunk layer"
    bp, sp, d = x_prompt.shape
    bs, ss, _ = x_sample.shape
    n_mem = mem_prompt.shape[1]
    l = 0

    wt = jnp.swapaxes(w_in, 1, 2)[l]
    o_ff = 3 * FOX_WIDTH
    o_g = o_ff + FOX_HEADS
    o_ga = o_g + GLA_PROJ_WIDTH
    w = {
        "norm_mix": norm_mix[l],
        "w_fox": wt[:o_ff].astype(BF16),
        "w_gla": wt[o_g:o_ga].astype(BF16),
        "w_small": jnp.concatenate([wt[o_ff:o_g], wt[o_ga:],
                                    jnp.zeros((LANES - FOX_HEADS - GLA_GATE_RANK, d), F32)], axis=0).astype(BF16),
        "b_fgate": jnp.pad(b_fgate[l], (0, LANES - FOX_HEADS)).reshape(1, LANES),
        "wa_pad": jnp.pad(w_gla_a2[l], ((FOX_HEADS, LANES - FOX_HEADS - GLA_GATE_RANK), (0, 0))).astype(BF16),
        "b_gla_a": b_gla_a[l].reshape(1, GLA_KEY_WIDTH),
        "norm_gla_out": norm_gla_out[l].reshape(1, GLA_DV),
        "w_out": w_out[l].astype(BF16),
        "norm_mem_q": norm_mem_q[l].reshape(1, d),
        "w_mq": w_mq[l].astype(BF16),
        "w_mo": w_mo[l].astype(BF16),
        "norm_ffn": norm_ffn[l].reshape(1, d),
        "w_pq_t": w_pq[l].T.astype(BF16),
        "sub_keys": sub_keys[l].astype(BF16),
        "expert_u": expert_u[l].astype(BF16),
        "expert_v_t": expert_v[l].reshape(-1, PEER_EXPERT_CHUNK, d).transpose(0, 2, 1).astype(BF16),
    }

    m = _rmsnorm(mem_prompt.reshape(bp * n_mem, d), norm_mem_kv[l])
    (mk,) = _mm(m, w_mk[l].astype(BF16), (F32,))
    (mv,) = _mm(m, w_mv[l].astype(BF16), (F32,))

    hp, pp, fkp, fvp, lfp, gsp = _layer(x_prompt.reshape(bp * sp, d), w, mk, mv, bp, sp, n_mem, None)
    past = (cache_fox_k[l], cache_fox_v[l], cache_fox_logf[l], state_gla[l])
    hs, ps, fks, fvs, lfs, gss = _layer(x_sample.reshape(bs * ss, d), w,
                                        cache_mem_k[l].reshape(bs * n_mem, MEM_WIDTH),
                                        cache_mem_v[l].reshape(bs * n_mem, MEM_WIDTH), bs, ss, n_mem, past)

    g_fin = norm_final.reshape(1, d)
    y_prompt = _final(hp, pp, g_fin).reshape(bp, sp, d)
    y_sample = _final(hs, ps, g_fin).reshape(bs, ss, d)
    hshape_p = (1, bp, sp, FOX_HEADS, FOX_HEAD_DIM)
    hshape_s = (1, bs, ss, FOX_HEADS, FOX_HEAD_DIM)
    return (y_prompt, y_sample,
            fkp.reshape(hshape_p), fvp.reshape(hshape_p), lfp.reshape(1, bp, sp, FOX_HEADS), gsp[None],
            mk.reshape(1, bp, n_mem, MEM_HEADS, MEM_HEAD_DIM), mv.reshape(1, bp, n_mem, MEM_HEADS, MEM_HEAD_DIM),
            fks.reshape(hshape_s), fvs.reshape(hshape_s), lfs.reshape(1, bs, ss, FOX_HEADS), gss[None])
```

```python
import functools

import jax
import jax.numpy as jnp
from jax import lax
from jax.experimental import pallas as pl
from jax.experimental.pallas import tpu as pltpu

F32 = jnp.float32
BF16 = jnp.bfloat16

EPS = 1e-6
NEG_INF = -1e30

FOX_HEADS = 16
FOX_HEAD_DIM = 128
FOX_WIDTH = FOX_HEADS * FOX_HEAD_DIM
FOX_HEADS_PER_STEP = 8
LOG2E = 1.4426950408889634
FOX_QSCALE = FOX_HEAD_DIM ** -0.5 * LOG2E
GLA_HEADS = 4
GLA_DK = 256
GLA_DV = 512
GLA_KEY_WIDTH = GLA_HEADS * GLA_DK
GLA_WIDTH = GLA_HEADS * GLA_DV
GLA_GATE_RANK = 16
GLA_TAU = 16.0
GLA_CHUNK = 128
GLA_SUB = 32
GLA_HEADS_PER_STEP = 4
MEM_HEADS = 4
MEM_HEAD_DIM = 128
MEM_WIDTH = MEM_HEADS * MEM_HEAD_DIM
PEER_HEADS = 8
PEER_NKEYS = 128
PEER_HALF = 128
PEER_TOPK = 16
PEER_EXPERT_CHUNK = 512
LANES = 128
GLA_PROJ_WIDTH = 2 * GLA_KEY_WIDTH + 2 * GLA_WIDTH

COL_FQ, COL_FK, COL_FV = 0, FOX_WIDTH, 2 * FOX_WIDTH
COL_GQ = 0
COL_GK = COL_GQ + GLA_KEY_WIDTH
COL_GV = COL_GK + GLA_KEY_WIDTH
COL_GG = COL_GV + GLA_WIDTH

VMEM_LIMIT_BYTES = 52 * 2**20


def _pick(n, prefs):
    for p in prefs:
        if n % p == 0:
            return p
    raise ValueError(f"no tile in {prefs} divides {n}")


def _params(*sem):
    return pltpu.CompilerParams(dimension_semantics=sem, vmem_limit_bytes=VMEM_LIMIT_BYTES)


def _log_sigmoid(x):
    return -(jnp.maximum(-x, 0.0) + jnp.log1p(jnp.exp(-jnp.abs(x))))


def _split3(x):
    hi = x.astype(BF16)
    r1 = x - hi.astype(F32)
    mid = r1.astype(BF16)
    lo = (r1 - mid.astype(F32)).astype(BF16)
    return hi, mid, lo


def _rmsnorm_kernel(x_ref, g_ref, o_ref):
    x = x_ref[...]
    y = x * lax.rsqrt(jnp.mean(x * x, axis=-1, keepdims=True) + EPS)
    o_ref[...] = (y * g_ref[...]).astype(o_ref.dtype)


def _rmsnorm(x, g):
    t, d = x.shape
    tm = _pick(t, (512, 256, 128))
    return pl.pallas_call(
        _rmsnorm_kernel,
        grid=(t // tm,),
        in_specs=[pl.BlockSpec((tm, d), lambda i: (i, 0)), pl.BlockSpec((1, d), lambda i: (0, 0))],
        out_specs=pl.BlockSpec((tm, d), lambda i: (i, 0)),
        out_shape=jax.ShapeDtypeStruct((t, d), BF16),
        compiler_params=_params("parallel"),
        name="rmsnorm",
    )(x, g.reshape(1, d))


def _mm_kernel(a_ref, w_ref, *o_refs, scale, w_rows):
    if w_rows:
        r = lax.dot_general(a_ref[...], w_ref[...], (((1,), (1,)), ((), ())), preferred_element_type=F32)
    else:
        r = jnp.dot(a_ref[...], w_ref[...], preferred_element_type=F32)
    if scale is not None:
        r = r * scale
    for o_ref in o_refs:
        o_ref[...] = r.astype(o_ref.dtype)


def _mm(a, w, out_dtypes, col0=0, ncols=None, scale=None, w_rows=False):
    t, k = a.shape
    n = w.shape[0 if w_rows else 1] - col0 if ncols is None else ncols
    tm = _pick(t, (512, 256, 128))
    tn = _pick(n, (1024, 512, 256, 128))
    assert col0 % tn == 0
    j0 = col0 // tn
    if w_rows:
        w_spec = pl.BlockSpec((tn, k), lambda j, i: (j0 + j, 0))
    else:
        w_spec = pl.BlockSpec((k, tn), lambda j, i: (0, j0 + j))
    outs = pl.pallas_call(
        functools.partial(_mm_kernel, scale=scale, w_rows=w_rows),
        grid=(n // tn, t // tm),
        in_specs=[pl.BlockSpec((tm, k), lambda j, i: (i, 0)), w_spec],
        out_specs=[pl.BlockSpec((tm, tn), lambda j, i: (i, j)) for _ in out_dtypes],
        out_shape=[jax.ShapeDtypeStruct((t, n), dt) for dt in out_dtypes],
        compiler_params=_params("parallel", "parallel"),
        name="proj",
    )(a, w)
    return outs


def _norm_mm_kernel(x_ref, g_ref, w_ref, n_ref, o_ref, *, scale):
    x = x_ref[...]
    n = (x * lax.rsqrt(jnp.mean(x * x, axis=-1, keepdims=True) + EPS) * g_ref[...]).astype(BF16)
    n_ref[...] = n
    r = lax.dot_general(n, w_ref[...], (((1,), (1,)), ((), ())), preferred_element_type=F32)
    o_ref[...] = (r * scale).astype(o_ref.dtype)


def _norm_mm(x, g, w_t, ncols, scale):
    t, k = x.shape
    tm = _pick(t, (512, 256, 128))
    return pl.pallas_call(
        functools.partial(_norm_mm_kernel, scale=scale),
        grid=(t // tm,),
        in_specs=[pl.BlockSpec((tm, k), lambda i: (i, 0)), pl.BlockSpec((1, k), lambda i: (0, 0)),
                  pl.BlockSpec((ncols, k), lambda i: (0, 0), pipeline_mode=pl.Buffered(1))],
        out_specs=[pl.BlockSpec((tm, k), lambda i: (i, 0)), pl.BlockSpec((tm, ncols), lambda i: (i, 0))],
        out_shape=[jax.ShapeDtypeStruct((t, k), BF16), jax.ShapeDtypeStruct((t, ncols), BF16)],
        compiler_params=_params("parallel"),
        name="norm_proj",
    )(x, g.reshape(1, k), w_t)


def _small_kernel(n_ref, w_ref, b_ref, o_ref):
    z = lax.dot_general(n_ref[...], w_ref[...], (((1,), (1,)), ((), ())), preferred_element_type=F32)
    col = lax.broadcasted_iota(jnp.int32, z.shape, 1)
    o_ref[...] = jnp.where(col < FOX_HEADS, _log_sigmoid(z + b_ref[...]), z)


def _small_proj(n, w_small_t, b_fgate_pad):
    t, k = n.shape
    tm = _pick(t, (512, 256, 128))
    return pl.pallas_call(
        _small_kernel,
        grid=(t // tm,),
        in_specs=[pl.BlockSpec((tm, k), lambda i: (i, 0)), pl.BlockSpec((LANES, k), lambda i: (0, 0)),
                  pl.BlockSpec((1, LANES), lambda i: (0, 0))],
        out_specs=pl.BlockSpec((tm, LANES), lambda i: (i, 0)),
        out_shape=jax.ShapeDtypeStruct((t, LANES), F32),
        compiler_params=_params("parallel"),
        name="small_proj",
    )(n, w_small_t, b_fgate_pad)


def _cumsum_kernel(x_ref, o_ref, carry_ref, *, lb):
    @pl.when(pl.program_id(1) == 0)
    def _():
        carry_ref[...] = jnp.zeros_like(carry_ref)

    xt = x_ref[0].T
    r = lax.broadcasted_iota(jnp.int32, (lb, lb), 0)
    c = lax.broadcasted_iota(jnp.int32, (lb, lb), 1)
    tri = (r <= c).astype(BF16)
    acc = carry_ref[...]
    for part in _split3(xt):
        acc = acc + jnp.dot(part, tri, preferred_element_type=F32)
    o_ref[0] = acc
    carry_ref[...] = jnp.broadcast_to(acc[:, lb - 1:lb], carry_ref.shape)


def _cumsum_t(x, lb):
    b, l, _ = x.shape
    return pl.pallas_call(
        functools.partial(_cumsum_kernel, lb=lb),
        grid=(b, l // lb),
        in_specs=[pl.BlockSpec((1, lb, LANES), lambda i, j: (i, j, 0))],
        out_specs=pl.BlockSpec((1, LANES, lb), lambda i, j: (i, 0, j)),
        out_shape=jax.ShapeDtypeStruct((b, LANES, l), F32),
        scratch_shapes=[pltpu.VMEM((LANES, lb), F32)],
        compiler_params=_params("parallel", "arbitrary"),
        name="cumsum_logf",
    )(x)


def _fox_prompt_kernel(q_ref, k_ref, v_ref, c_ref, o_ref, m_sc, acc_sc, *, tq, hb):
    qi = pl.program_id(2)
    m_sc[...] = jnp.full_like(m_sc, NEG_INF)
    acc_sc[...] = jnp.zeros_like(acc_sc)

    def chunk(kc, masked):
        ks = pl.multiple_of(kc * tq, tq)
        ones = jnp.ones((tq, LANES), BF16)
        for hh in range(hb):
            cs = slice(hh * FOX_HEAD_DIM, (hh + 1) * FOX_HEAD_DIM)
            s = lax.dot_general(q_ref[:, cs], k_ref[pl.ds(ks, tq), cs], (((1,), (1,)), ((), ())),
                                preferred_element_type=F32) - c_ref[hh, kc] * LOG2E
            if masked:
                r = lax.broadcasted_iota(jnp.int32, s.shape, 0)
                c = lax.broadcasted_iota(jnp.int32, s.shape, 1)
                s = jnp.where(r >= c, s, NEG_INF)
            m_prev = m_sc[hh]
            m_new = jnp.maximum(m_prev, jnp.max(s, axis=1, keepdims=True))
            alpha = jnp.exp2(m_prev - m_new)
            p = jnp.exp2(s - jnp.concatenate([m_new] * (tq // LANES), axis=1))
            v_ext = jnp.concatenate([v_ref[pl.ds(ks, tq), cs], ones], axis=1)
            pv = jnp.dot(p.astype(BF16), v_ext, preferred_element_type=F32)
            acc_sc[hh] = jnp.concatenate([alpha, alpha], axis=1) * acc_sc[hh] + pv
            m_sc[hh] = m_new

    def body(kc, carry):
        chunk(kc, False)
        return carry

    lax.fori_loop(0, qi, body, 0)
    chunk(qi, True)
    for hh in range(hb):
        acc = acc_sc[hh]
        o_ref[:, hh * FOX_HEAD_DIM:(hh + 1) * FOX_HEAD_DIM] = (
            acc[:, :FOX_HEAD_DIM] / acc[:, FOX_HEAD_DIM:]).astype(o_ref.dtype)


def _fox_prompt(q16, k16, v16, c4, batch, seq):
    tq = _pick(seq, (512, 256, 128))
    nq = seq // tq
    hb = FOX_HEADS_PER_STEP
    wid = hb * FOX_HEAD_DIM
    return pl.pallas_call(
        functools.partial(_fox_prompt_kernel, tq=tq, hb=hb),
        grid=(batch, FOX_HEADS // hb, nq),
        in_specs=[
            pl.BlockSpec((tq, wid), lambda b, h, i: (b * nq + i, h)),
            pl.BlockSpec((seq, wid), lambda b, h, i: (b, h)),
            pl.BlockSpec((seq, wid), lambda b, h, i: (b, h)),
            pl.BlockSpec((hb, nq, 1, tq), lambda b, h, i: (b * (LANES // hb) + h, 0, 0, 0)),
        ],
        out_specs=pl.BlockSpec((tq, wid), lambda b, h, i: (b * nq + i, h)),
        out_shape=jax.ShapeDtypeStruct((batch * seq, FOX_WIDTH), BF16),
        scratch_shapes=[pltpu.VMEM((hb, tq, LANES), F32), pltpu.VMEM((hb, tq, 2 * FOX_HEAD_DIM), F32)],
        compiler_params=_params("parallel", "parallel", "arbitrary"),
        name="fox_prompt",
    )(q16, k16, v16, c4)


def _fox_sample_kernel(q_ref, kn_ref, vn_ref, ck_ref, cv_ref, c_ref, o_ref, *, past, new):
    pad = jnp.zeros((LANES - new, FOX_HEAD_DIM), BF16)
    dn = (((1,), (1,)), ((), ()))
    r = lax.broadcasted_iota(jnp.int32, (new, LANES), 0)
    cc = lax.broadcasted_iota(jnp.int32, (new, LANES), 1)
    for hd in range(FOX_HEADS):
        cs = slice(hd * FOX_HEAD_DIM, (hd + 1) * FOX_HEAD_DIM)
        q = q_ref[:, cs]
        kn = jnp.concatenate([kn_ref[:, cs], pad], axis=0)
        vn = jnp.concatenate([vn_ref[:, cs], pad], axis=0)
        c = c_ref[hd] * LOG2E
        ck = ck_ref[0, pl.ds(hd, past, stride=FOX_HEADS), :].astype(BF16)
        cv = cv_ref[0, pl.ds(hd, past, stride=FOX_HEADS), :].astype(BF16)
        s1 = lax.dot_general(q, ck, dn, preferred_element_type=F32) - c[:, :past]
        s2 = lax.dot_general(q, kn, dn, preferred_element_type=F32) - c[:, past:]
        s2 = jnp.where(r >= cc, s2, NEG_INF)
        m = jnp.maximum(jnp.max(s1, axis=1, keepdims=True), jnp.max(s2, axis=1, keepdims=True))
        p1 = jnp.exp2(s1 - m)
        p2 = jnp.exp2(s2 - m)
        l = jnp.sum(p1, axis=1, keepdims=True) + jnp.sum(p2, axis=1, keepdims=True)
        o = (jnp.dot(p1.astype(BF16), cv, preferred_element_type=F32)
             + jnp.dot(p2.astype(BF16), vn, preferred_element_type=F32))
        o_ref[:, cs] = (o / l).astype(o_ref.dtype)


def _fox_sample(q16, k16, v16, ck, cv, c3, batch, new, past):
    cache = pl.BlockSpec((1, past * FOX_HEADS, FOX_HEAD_DIM), lambda b: (b, 0, 0))
    rows = pl.BlockSpec((new, FOX_WIDTH), lambda b: (b, 0))
    return pl.pallas_call(
        functools.partial(_fox_sample_kernel, past=past, new=new),
        grid=(batch,),
        in_specs=[rows, rows, rows, cache, cache,
                  pl.BlockSpec((FOX_HEADS, 1, past + LANES), lambda b: (b * (LANES // FOX_HEADS), 0, 0))],
        out_specs=rows,
        out_shape=jax.ShapeDtypeStruct((batch * new, FOX_WIDTH), BF16),
        compiler_params=_params("parallel"),
        name="fox_sample",
    )(q16, k16, v16, ck, cv, c3)


def _gla_kernel(q_ref, k_ref, v_ref, gg_ref, sm_ref, wa_ref, ba_ref, g_ref, s0_ref, go_ref, sout_ref, s_sc,
                *, valid, has_init, hb):
    ci = pl.program_id(2)

    @pl.when(ci == 0)
    def _():
        if has_init:
            s_sc[...] = s0_ref[0]
        else:
            s_sc[...] = jnp.zeros_like(s_sc)

    for hh in range(hb):
        ks = slice(hh * GLA_DK, (hh + 1) * GLA_DK)
        vs = slice(hh * GLA_DV, (hh + 1) * GLA_DV)
        _gla_chunk(q_ref.at[:, ks], k_ref.at[:, ks], v_ref.at[:, vs], gg_ref.at[:, vs], sm_ref,
                   wa_ref.at[:, ks], ba_ref.at[:, ks], g_ref, s_sc.at[hh], go_ref.at[:, vs], valid)

    @pl.when(ci == pl.num_programs(2) - 1)
    def _():
        sout_ref[0] = s_sc[...]


def _gla_chunk(q_ref, k_ref, v_ref, gg_ref, sm_ref, wa_ref, ba_ref, g_ref, s_ref, go_ref, valid):
    C = GLA_CHUNK

    def rows(ref, dtype):
        x = ref[...].astype(dtype)
        if valid < C:
            x = jnp.concatenate([x, jnp.zeros((C - valid, x.shape[1]), dtype)], axis=0)
        return x

    row = lax.broadcasted_iota(jnp.int32, (C, C), 0)
    col = lax.broadcasted_iota(jnp.int32, (C, C), 1)

    a = jnp.dot(rows(sm_ref, BF16), wa_ref[...], preferred_element_type=F32) + ba_ref[...]
    lg = _log_sigmoid(a) * (1.0 / GLA_TAU)
    if valid < C:
        lg = jnp.where(lax.broadcasted_iota(jnp.int32, lg.shape, 0) < valid, lg, 0.0)
    low = (row >= col).astype(BF16)
    bc = jnp.zeros_like(lg)
    for part in _split3(lg):
        bc = bc + jnp.dot(low, part, preferred_element_type=F32)

    q = rows(q_ref, F32) * (GLA_DK ** -0.5)
    k = rows(k_ref, F32)
    v = rows(v_ref, BF16)
    s = s_ref[...]

    o_state = jnp.dot((q * jnp.exp(bc)).astype(BF16), s.astype(BF16), preferred_element_type=F32)
    krow = lax.broadcasted_iota(jnp.int32, (C, GLA_DK), 0)
    sub_row = lax.broadcasted_iota(jnp.int32, (GLA_SUB, C), 0)
    sub_col = lax.broadcasted_iota(jnp.int32, (GLA_SUB, C), 1)
    outs = []
    for i in range(C // GLA_SUB):
        r0, r1 = i * GLA_SUB, (i + 1) * GLA_SUB
        if valid <= r0:
            outs.append(o_state[r0:r1])
            continue
        base = bc[r0 - 1:r0] if i > 0 else jnp.zeros((1, GLA_DK), F32)
        qt = (q[r0:r1] * jnp.exp(bc[r0:r1] - base)).astype(BF16)
        kt = jnp.where(krow < r1, k * jnp.exp(base - bc), 0.0).astype(BF16)
        att = lax.dot_general(qt, kt, (((1,), (1,)), ((), ())), preferred_element_type=F32)
        att = jnp.where(sub_col <= sub_row + r0, att, 0.0)
        outs.append(o_state[r0:r1] + jnp.dot(att.astype(BF16), v, preferred_element_type=F32))
    o = jnp.concatenate(outs, axis=0)

    b_end = bc[C - 1:C]
    kd = k * jnp.exp(b_end - bc)
    decay = jnp.broadcast_to(jnp.exp(b_end), (LANES, GLA_DK)).T
    decay = jnp.concatenate([decay] * (GLA_DV // LANES), axis=1)
    s_new = decay * s + jnp.dot(kd.T.astype(BF16), v, preferred_element_type=F32)
    s_ref[...] = s_new

    o = o[:valid]
    on = o * lax.rsqrt(jnp.mean(o * o, axis=-1, keepdims=True) + EPS) * g_ref[...]
    gg = gg_ref[...]
    go_ref[...] = (on * (gg * jax.nn.sigmoid(gg))).astype(go_ref.dtype)


def _gla(z32, z16, small, wa_pad, b_a, g_out, s0, batch, tokens):
    if tokens >= GLA_CHUNK:
        assert tokens % GLA_CHUNK == 0
        blk, nch = GLA_CHUNK, tokens // GLA_CHUNK
    else:
        blk, nch = tokens, 1
    hb = GLA_HEADS_PER_STEP
    kw, vw = hb * GLA_DK, hb * GLA_DV
    has_init = s0 is not None
    if s0 is None:
        s0 = jnp.zeros((1, hb, GLA_DK, GLA_DV), F32)
        s0_map = lambda b, h, c: (0, 0, 0, 0)
    else:
        s0_map = lambda b, h, c: (b, h, 0, 0)
    qb, kb = COL_GQ // kw, COL_GK // kw
    vb, gb = COL_GV // vw, COL_GG // vw
    go, s_out = pl.pallas_call(
        functools.partial(_gla_kernel, valid=blk, has_init=has_init, hb=hb),
        grid=(batch, GLA_HEADS // hb, nch),
        in_specs=[
            pl.BlockSpec((blk, kw), lambda b, h, c: (b * nch + c, qb + h)),
            pl.BlockSpec((blk, kw), lambda b, h, c: (b * nch + c, kb + h)),
            pl.BlockSpec((blk, vw), lambda b, h, c: (b * nch + c, vb + h)),
            pl.BlockSpec((blk, vw), lambda b, h, c: (b * nch + c, gb + h)),
            pl.BlockSpec((blk, LANES), lambda b, h, c: (b * nch + c, 0)),
            pl.BlockSpec((LANES, kw), lambda b, h, c: (0, h)),
            pl.BlockSpec((1, kw), lambda b, h, c: (0, h)),
            pl.BlockSpec((1, GLA_DV), lambda b, h, c: (0, 0)),
            pl.BlockSpec((1, hb, GLA_DK, GLA_DV), s0_map),
        ],
        out_specs=[
            pl.BlockSpec((blk, vw), lambda b, h, c: (b * nch + c, h)),
            pl.BlockSpec((1, hb, GLA_DK, GLA_DV), lambda b, h, c: (b, h, 0, 0)),
        ],
        out_shape=[jax.ShapeDtypeStruct((batch * tokens, GLA_WIDTH), BF16),
                   jax.ShapeDtypeStruct((batch, GLA_HEADS, GLA_DK, GLA_DV), F32)],
        scratch_shapes=[pltpu.VMEM((hb, GLA_DK, GLA_DV), F32)],
        compiler_params=_params("parallel", "parallel", "arbitrary"),
        name="gla",
    )(z32, z32, z16, z32, small, wa_pad, b_a, g_out, s0)
    return go, s_out


def _outproj_kernel(fo_ref, go_ref, wt_ref, wb_ref, x_ref, o_ref):
    o_ref[...] = (x_ref[...] + jnp.dot(fo_ref[...], wt_ref[...], preferred_element_type=F32)
                  + jnp.dot(go_ref[...], wb_ref[...], preferred_element_type=F32))


def _outproj(fo, go, w_out, x):
    t, d = x.shape
    half = fo.shape[1]
    tm = _pick(t, (512, 256, 128))
    tn = _pick(d, (1024, 512))
    return pl.pallas_call(
        _outproj_kernel,
        grid=(d // tn, t // tm),
        in_specs=[
            pl.BlockSpec((tm, half), lambda j, i: (i, 0)),
            pl.BlockSpec((tm, half), lambda j, i: (i, 0)),
            pl.BlockSpec((half, tn), lambda j, i: (0, j)),
            pl.BlockSpec((half, tn), lambda j, i: (1, j)),
            pl.BlockSpec((tm, tn), lambda j, i: (i, j)),
        ],
        out_specs=pl.BlockSpec((tm, tn), lambda j, i: (i, j)),
        out_shape=jax.ShapeDtypeStruct((t, d), F32),
        compiler_params=_params("parallel", "parallel"),
        name="outproj",
    )(fo, go, w_out, w_out, x)


def _mem_kernel(h_ref, gq_ref, wq_ref, mk_ref, mv_ref, wo_ref, gf_ref, h2_ref, n3t_ref, *, per_batch, n_mem):
    h = h_ref[...]
    tm = h.shape[0]
    n2 = (h * lax.rsqrt(jnp.mean(h * h, axis=-1, keepdims=True) + EPS) * gq_ref[...]).astype(BF16)
    q = jnp.dot(n2, wq_ref[...], preferred_element_type=F32)
    scale = MEM_HEAD_DIM ** -0.5
    rows_per = min(per_batch, tm)
    parts = []
    for b in range(tm // rows_per):
        heads = []
        for hd in range(MEM_HEADS):
            cs = slice(hd * MEM_HEAD_DIM, (hd + 1) * MEM_HEAD_DIM)
            qh = q[b * rows_per:(b + 1) * rows_per, cs].astype(BF16)
            kh = mk_ref[b * n_mem:(b + 1) * n_mem, cs].astype(BF16)
            vh = mv_ref[b * n_mem:(b + 1) * n_mem, cs].astype(BF16)
            s = lax.dot_general(qh, kh, (((1,), (1,)), ((), ())), preferred_element_type=F32) * scale
            s = s - jnp.max(s, axis=1, keepdims=True)
            p = jnp.exp(s)
            p = p / jnp.sum(p, axis=1, keepdims=True)
            heads.append(jnp.dot(p.astype(BF16), vh, preferred_element_type=F32))
        parts.append(jnp.concatenate(heads, axis=1))
    o = jnp.concatenate(parts, axis=0).astype(BF16)
    h2 = h + jnp.dot(o, wo_ref[...], preferred_element_type=F32)
    h2_ref[...] = h2
    n3 = h2 * lax.rsqrt(jnp.mean(h2 * h2, axis=-1, keepdims=True) + EPS) * gf_ref[...]
    n3t_ref[...] = n3.T.astype(BF16)


def _mem_block(h1, g_q, w_mq, mk, mv, w_mo, g_ffn, per_batch, n_mem):
    t, d = h1.shape
    tm = _pick(t, (256, 128)) if per_batch >= 256 else LANES
    nb = max(tm // per_batch, 1)
    tiles_per_batch = max(per_batch // tm, 1)
    return pl.pallas_call(
        functools.partial(_mem_kernel, per_batch=per_batch, n_mem=n_mem),
        grid=(t // tm,),
        in_specs=[
            pl.BlockSpec((tm, d), lambda i: (i, 0)),
            pl.BlockSpec((1, d), lambda i: (0, 0)),
            pl.BlockSpec((d, MEM_WIDTH), lambda i: (0, 0)),
            pl.BlockSpec((nb * n_mem, MEM_WIDTH), lambda i: (i // tiles_per_batch, 0)),
            pl.BlockSpec((nb * n_mem, MEM_WIDTH), lambda i: (i // tiles_per_batch, 0)),
            pl.BlockSpec((MEM_WIDTH, d), lambda i: (0, 0)),
            pl.BlockSpec((1, d), lambda i: (0, 0)),
        ],
        out_specs=[pl.BlockSpec((tm, d), lambda i: (i, 0)), pl.BlockSpec((d, tm), lambda i: (0, i))],
        out_shape=[jax.ShapeDtypeStruct((t, d), F32), jax.ShapeDtypeStruct((d, t), BF16)],
        compiler_params=_params("parallel"),
        name="mem_attn",
    )(h1, g_q, w_mq, mk, mv, w_mo, g_ffn)


def _extract_top(vals, key, count, exact):
    cur = vals
    rank = jnp.full(vals.shape, float(count), F32)
    tops = []
    big = jnp.float32(2**30)
    for r in range(count):
        m = jnp.max(cur, axis=0, keepdims=True)
        hit = cur == m
        if exact:
            first = jnp.min(jnp.where(hit, key, big), axis=0, keepdims=True)
            hit = key == first
        rank = jnp.where(hit, float(r), rank)
        cur = jnp.where(hit, -jnp.inf, cur)
        tops.append(m)
    return tops, rank


def _peer_route_kernel(w_ref, n3t_ref, keys_ref, pk2_ref, pk1_ref, qp_sc):
    qp_sc[...] = jnp.dot(w_ref[...], n3t_ref[...], preferred_element_type=F32)
    qd = 2 * PEER_HALF

    def head(hd, carry):
        r0 = pl.multiple_of(hd * qd, qd)
        s1 = jnp.dot(keys_ref[0], qp_sc[pl.ds(r0, PEER_HALF), :].astype(BF16), preferred_element_type=F32)
        s2 = jnp.dot(keys_ref[1], qp_sc[pl.ds(r0 + PEER_HALF, PEER_HALF), :].astype(BF16),
                     preferred_element_type=F32)

        def store(tables):
            rank2, e2, cnt, e1 = tables
            pk2_ref[hd, 0] = rank2.astype(BF16)
            pk2_ref[hd, 1] = e2.astype(BF16)
            pk1_ref[hd, 0] = cnt
            pk1_ref[hd, 1] = e1

        tables, removed = _route_tables(s1, s2, exact=False)
        store(tables)
        ties = jnp.max(jnp.abs(removed - float(3 * PEER_TOPK)))

        @pl.when(ties > 0.0)
        def _():
            store(_route_tables(s1, s2, exact=True)[0])

        return carry

    lax.fori_loop(0, PEER_HEADS, head, 0)


def _route_tables(s1, s2, exact):
    K = PEER_TOPK
    tm = s1.shape[1]
    kiota = lax.broadcasted_iota(jnp.int32, s1.shape, 0).astype(F32)
    a, rank1 = _extract_top(s1, kiota, K, exact)
    b, rank2 = _extract_top(s2, kiota, K, exact)

    half = K // 2
    bmat = jnp.concatenate(b, axis=0)
    jio = lax.broadcasted_iota(jnp.int32, (K, tm), 0).astype(F32)
    jio_h = lax.broadcasted_iota(jnp.int32, (half, tm), 0).astype(F32)
    blocks, keys, spans = [a[0] + bmat], [jio], [(0, K)]
    for i in range(1, half):
        blocks.append(jnp.where(jio_h < float(K // (i + 1)), a[i] + bmat[:half], -jnp.inf))
        keys.append(jio_h + float(i * K))
        spans.append((K + (i - 1) * half, K + i * half))
    blocks.append(jnp.concatenate([a[i] + b[0] for i in range(half, K)], axis=0))
    keys.append((jio_h + float(half)) * float(K))
    tail = K + (half - 1) * half
    spans += [(tail + i, tail + i + 1) for i in range(K - half)]
    cand = jnp.concatenate(blocks, axis=0)
    ckey = jnp.concatenate(keys, axis=0)
    _, crank = _extract_top(cand, ckey, K, exact)
    sel = crank < float(K)
    z = jnp.sum(jnp.where(sel, jnp.exp(cand - (a[0] + b[0])), 0.0), axis=0, keepdims=True)
    self32 = sel.astype(F32)
    counts = [jnp.sum(self32[lo:hi], axis=0, keepdims=True) for lo, hi in spans]

    cnt = jnp.zeros(s1.shape, F32)
    for i in range(K):
        cnt = jnp.where(rank1 == float(i), counts[i], cnt)
    removed = (jnp.sum((rank1 < float(K)).astype(F32), axis=0, keepdims=True)
               + jnp.sum((rank2 < float(K)).astype(F32), axis=0, keepdims=True)
               + jnp.sum(self32, axis=0, keepdims=True))
    return (rank2, jnp.exp(s2 - b[0]), cnt, jnp.exp(s1 - a[0]) / z), removed


def _peer_route(w_pq_t, n3t, keys):
    d, t = n3t.shape
    tm = _pick(t, (512, 256, 128))
    qw = PEER_HEADS * 2 * PEER_HALF
    spec = pl.BlockSpec((PEER_HEADS, 2, PEER_NKEYS, tm), lambda i: (0, 0, 0, i))
    return pl.pallas_call(
        _peer_route_kernel,
        grid=(t // tm,),
        in_specs=[
            pl.BlockSpec((qw, d), lambda i: (0, 0), pipeline_mode=pl.Buffered(1)),
            pl.BlockSpec((d, tm), lambda i: (0, i)),
            pl.BlockSpec((2, PEER_NKEYS, PEER_HALF), lambda i: (0, 0, 0)),
        ],
        out_specs=[spec, spec],
        out_shape=[jax.ShapeDtypeStruct((PEER_HEADS, 2, PEER_NKEYS, t), BF16),
                   jax.ShapeDtypeStruct((PEER_HEADS, 2, PEER_NKEYS, t), F32)],
        scratch_shapes=[pltpu.VMEM((qw, tm), F32)],
        compiler_params=_params("parallel"),
        name="peer_route",
    )(w_pq_t, n3t, keys)


def _gelu_tanh(x):
    inner = x * (0.7978845608028654 + (0.7978845608028654 * 0.044715) * (x * x))
    half = 0.5 * x
    return half + half * jnp.tanh(inner)


def _peer_kernel(n3t_ref, pk2_ref, pk1_ref, u_ref, vt_ref, o_ref, act_sc, *, ec, n_chunks):
    s = pl.program_id(0)
    cur = s % 2
    acc_pair = jnp.maximum(s - 1, 0)
    act_pair = jnp.minimum(s, pl.num_programs(0) - 2)

    @pl.when(s == 0)
    def _():
        act_sc[1] = jnp.zeros(act_sc.shape[1:], act_sc.dtype)

    @pl.when(acc_pair % n_chunks == 0)
    def _():
        o_ref[...] = jnp.zeros_like(o_ref)

    h = jnp.dot(u_ref[...], n3t_ref[...], preferred_element_type=F32)
    o_ref[...] += jnp.dot(vt_ref[0], act_sc[1 - cur], preferred_element_type=F32)
    k1_base = (act_pair % n_chunks) * (ec // PEER_NKEYS)
    for kk in range(ec // PEER_NKEYS):
        rows = slice(kk * PEER_NKEYS, (kk + 1) * PEER_NKEYS)
        gate = jnp.zeros((PEER_NKEYS, h.shape[1]), BF16)
        for hd in range(PEER_HEADS):
            cnt = pk1_ref[hd, 0, pl.ds(k1_base + kk, 1), :].astype(BF16)
            e1 = pk1_ref[hd, 1, pl.ds(k1_base + kk, 1), :].astype(BF16)
            gate = gate + jnp.where(pk2_ref[hd, 0] < cnt, pk2_ref[hd, 1], 0.0) * e1
        act_sc[cur, rows, :] = (_gelu_tanh(h[rows]) * gate.astype(F32)).astype(BF16)


def _peer(n3t, pk2, pk1, u, vt):
    d, t = n3t.shape
    n_chunks, _, ec = vt.shape
    tm = _pick(t, (512, 256, 128))
    once = pl.Buffered(1)
    n_pairs = (t // tm) * n_chunks
    act_pair = lambda s: jnp.minimum(s, n_pairs - 1)
    acc_pair = lambda s: jnp.maximum(s - 1, 0)
    return pl.pallas_call(
        functools.partial(_peer_kernel, ec=ec, n_chunks=n_chunks),
        grid=(n_pairs + 1,),
        in_specs=[
            pl.BlockSpec((d, tm), lambda s: (0, act_pair(s) // n_chunks), pipeline_mode=once),
            pl.BlockSpec((PEER_HEADS, 2, PEER_NKEYS, tm), lambda s: (0, 0, 0, act_pair(s) // n_chunks),
                         pipeline_mode=once),
            pl.BlockSpec((PEER_HEADS, 2, PEER_NKEYS, tm), lambda s: (0, 0, 0, act_pair(s) // n_chunks),
                         pipeline_mode=once),
            pl.BlockSpec((ec, d), lambda s: (act_pair(s) % n_chunks, 0)),
            pl.BlockSpec((1, d, ec), lambda s: (acc_pair(s) % n_chunks, 0, 0)),
        ],
        out_specs=pl.BlockSpec((d, tm), lambda s: (0, acc_pair(s) // n_chunks)),
        out_shape=jax.ShapeDtypeStruct((d, t), F32),
        scratch_shapes=[pltpu.VMEM((2, ec, tm), BF16)],
        compiler_params=_params("arbitrary"),
        name="peer_experts",
    )(n3t, pk2, pk1, u, vt)


def _final_kernel(h_ref, pt_ref, g_ref, o_ref):
    x = h_ref[...] + pt_ref[...].T
    y = x * lax.rsqrt(jnp.mean(x * x, axis=-1, keepdims=True) + EPS)
    o_ref[...] = y * g_ref[...]


def _final(h2, peer_t, g):
    t, d = h2.shape
    tm = _pick(t, (256, 128))
    return pl.pallas_call(
        _final_kernel,
        grid=(t // tm,),
        in_specs=[pl.BlockSpec((tm, d), lambda i: (i, 0)), pl.BlockSpec((d, tm), lambda i: (0, i)),
                  pl.BlockSpec((1, d), lambda i: (0, 0))],
        out_specs=pl.BlockSpec((tm, d), lambda i: (i, 0)),
        out_shape=jax.ShapeDtypeStruct((t, d), F32),
        compiler_params=_params("parallel"),
        name="final_norm",
    )(h2, peer_t, g)


def _layer(x, w, mem_k, mem_v, batch, tokens, n_mem, past):
    n, q16 = _norm_mm(x, w["norm_mix"], w["w_fox"], FOX_WIDTH, FOX_QSCALE)
    fk, k16 = _mm(n, w["w_fox"], (F32, BF16), COL_FK, FOX_WIDTH, w_rows=True)
    fv, v16 = _mm(n, w["w_fox"], (F32, BF16), COL_FV, FOX_WIDTH, w_rows=True)
    z32, z16 = _mm(n, w["w_gla"], (F32, BF16), w_rows=True)
    small = _small_proj(n, w["w_small"], w["b_fgate"])
    logf = small[:, :FOX_HEADS]

    if past is None:
        lb = _pick(tokens, (512, 256, 128))
        c = _cumsum_t(small.reshape(batch, tokens, LANES), lb)
        tq = _pick(tokens, (512, 256, 128))
        c4 = c.reshape(batch * LANES, tokens // tq, 1, tq)
        fo = _fox_prompt(q16, k16, v16, c4, batch, tokens)
        go, gla_state = _gla(z32, z16, small, w["wa_pad"], w["b_gla_a"], w["norm_gla_out"], None, batch, tokens)
    else:
        ck, cv, clogf, s0 = past
        plen = ck.shape[1]
        lf = jnp.concatenate([
            jnp.pad(clogf.astype(F32), ((0, 0), (0, 0), (0, LANES - FOX_HEADS))),
            small.reshape(batch, tokens, LANES),
            jnp.zeros((batch, LANES - tokens, LANES), F32)], axis=1)
        c = _cumsum_t(lf, LANES)
        c3 = c.reshape(batch * LANES, 1, plen + LANES)
        fo = _fox_sample(q16, k16, v16, ck.reshape(batch, plen * FOX_HEADS, FOX_HEAD_DIM),
                         cv.reshape(batch, plen * FOX_HEADS, FOX_HEAD_DIM), c3, batch, tokens, plen)
        go, gla_state = _gla(z32, z16, small, w["wa_pad"], w["b_gla_a"], w["norm_gla_out"], s0.astype(F32),
                             batch, tokens)

    h1 = _outproj(fo, go, w["w_out"], x)
    h2, n3t = _mem_block(h1, w["norm_mem_q"], w["w_mq"], mem_k, mem_v, w["w_mo"], w["norm_ffn"], tokens, n_mem)
    pk2, pk1 = _peer_route(w["w_pq_t"], n3t, w["sub_keys"])
    peer_t = _peer(n3t, pk2, pk1, w["expert_u"], w["expert_v_t"])
    return h2, peer_t, fk, fv, logf, gla_state


def kernel(x_prompt, x_sample, cache_fox_k, cache_fox_v, cache_fox_logf, state_gla, cache_mem_k, cache_mem_v, mem_prompt, norm_mix, w_in, b_fgate, w_gla_a2, b_gla_a, norm_gla_out, w_out, norm_mem_q, norm_mem_kv, w_mq, w_mk, w_mv, w_mo, norm_ffn, w_pq, sub_keys, expert_u, expert_v, norm_final):
    depth = w_in.shape[0]
    assert depth == 1, "one trunk layer"
    bp, sp, d = x_prompt.shape
    bs, ss, _ = x_sample.shape
    n_mem = mem_prompt.shape[1]
    l = 0

    wt = jnp.swapaxes(w_in, 1, 2)[l]
    o_ff = 3 * FOX_WIDTH
    o_g = o_ff + FOX_HEADS
    o_ga = o_g + GLA_PROJ_WIDTH
    w = {
        "norm_mix": norm_mix[l],
        "w_fox": wt[:o_ff].astype(BF16),
        "w_gla": wt[o_g:o_ga].astype(BF16),
        "w_small": jnp.concatenate([wt[o_ff:o_g], wt[o_ga:],
                                    jnp.zeros((LANES - FOX_HEADS - GLA_GATE_RANK, d), F32)], axis=0).astype(BF16),
        "b_fgate": jnp.pad(b_fgate[l], (0, LANES - FOX_HEADS)).reshape(1, LANES),
        "wa_pad": jnp.pad(w_gla_a2[l], ((FOX_HEADS, LANES - FOX_HEADS - GLA_GATE_RANK), (0, 0))).astype(BF16),
        "b_gla_a": b_gla_a[l].reshape(1, GLA_KEY_WIDTH),
        "norm_gla_out": norm_gla_out[l].reshape(1, GLA_DV),
        "w_out": w_out[l].astype(BF16),
        "norm_mem_q": norm_mem_q[l].reshape(1, d),
        "w_mq": w_mq[l].astype(BF16),
        "w_mo": w_mo[l].astype(BF16),
        "norm_ffn": norm_ffn[l].reshape(1, d),
        "w_pq_t": w_pq[l].T.astype(BF16),
        "sub_keys": sub_keys[l].astype(BF16),
        "expert_u": expert_u[l].astype(BF16),
        "expert_v_t": expert_v[l].reshape(-1, PEER_EXPERT_CHUNK, d).transpose(0, 2, 1).astype(BF16),
    }

    m = _rmsnorm(mem_prompt.reshape(bp * n_mem, d), norm_mem_kv[l])
    (mk,) = _mm(m, w_mk[l].astype(BF16), (F32,))
    (mv,) = _mm(m, w_mv[l].astype(BF16), (F32,))

    hp, pp, fkp, fvp, lfp, gsp = _layer(x_prompt.reshape(bp * sp, d), w, mk, mv, bp, sp, n_mem, None)
    past = (cache_fox_k[l], cache_fox_v[l], cache_fox_logf[l], state_gla[l])
    hs, ps, fks, fvs, lfs, gss = _layer(x_sample.reshape(bs * ss, d), w,
                                        cache_mem_k[l].reshape(bs * n_mem, MEM_WIDTH),
                                        cache_mem_v[l].reshape(bs * n_mem, MEM_WIDTH), bs, ss, n_mem, past)

    g_fin = norm_final.reshape(1, d)
    y_prompt = _final(hp, pp, g_fin).reshape(bp, sp, d)
    y_sample = _final(hs, ps, g_fin).reshape(bs, ss, d)
    hshape_p = (1, bp, sp, FOX_HEADS, FOX_HEAD_DIM)
    hshape_s = (1, bs, ss, FOX_HEADS, FOX_HEAD_DIM)
    return (y_prompt, y_sample,
            fkp.reshape(hshape_p), fvp.reshape(hshape_p), lfp.reshape(1, bp, sp, FOX_HEADS), gsp[None],
            mk.reshape(1, bp, n_mem, MEM_HEADS, MEM_HEAD_DIM), mv.reshape(1, bp, n_mem, MEM_HEADS, MEM_HEAD_DIM),
            fks.reshape(hshape_s), fvs.reshape(hshape_s), lfs.reshape(1, bs, ss, FOX_HEADS), gss[None])
```

```python
import functools

import jax
import jax.numpy as jnp
from jax import lax
from jax.experimental import pallas as pl
from jax.experimental.pallas import tpu as pltpu

F32 = jnp.float32
BF16 = jnp.bfloat16

EPS = 1e-6
NEG_INF = -1e30

FOX_HEADS = 16
FOX_HEAD_DIM = 128
FOX_WIDTH = FOX_HEADS * FOX_HEAD_DIM
FOX_HEADS_PER_STEP = 8
LOG2E = 1.4426950408889634
FOX_QSCALE = FOX_HEAD_DIM ** -0.5 * LOG2E
GLA_HEADS = 4
GLA_DK = 256
GLA_DV = 512
GLA_KEY_WIDTH = GLA_HEADS * GLA_DK
GLA_WIDTH = GLA_HEADS * GLA_DV
GLA_GATE_RANK = 16
GLA_TAU = 16.0
GLA_CHUNKS = (256, 128)
GLA_SUB = 32
GLA_HEADS_PER_STEP = 4
MEM_HEADS = 4
MEM_HEAD_DIM = 128
MEM_WIDTH = MEM_HEADS * MEM_HEAD_DIM
PEER_HEADS = 8
PEER_NKEYS = 128
PEER_HALF = 128
PEER_TOPK = 16
PEER_EXPERT_CHUNK = 512
LANES = 128
GLA_PROJ_WIDTH = 2 * GLA_KEY_WIDTH + 2 * GLA_WIDTH

COL_FQ, COL_FK, COL_FV = 0, FOX_WIDTH, 2 * FOX_WIDTH
COL_GQ = 0
COL_GK = COL_GQ + GLA_KEY_WIDTH
COL_GV = COL_GK + GLA_KEY_WIDTH
COL_GG = COL_GV + GLA_WIDTH

VMEM_LIMIT_BYTES = 52 * 2**20


def _pick(n, prefs):
    for p in prefs:
        if n % p == 0:
            return p
    raise ValueError(f"no tile in {prefs} divides {n}")


def _params(*sem):
    return pltpu.CompilerParams(dimension_semantics=sem, vmem_limit_bytes=VMEM_LIMIT_BYTES)


def _log_sigmoid(x):
    return -(jnp.maximum(-x, 0.0) + jnp.log1p(jnp.exp(-jnp.abs(x))))


def _split3(x):
    hi = x.astype(BF16)
    r1 = x - hi.astype(F32)
    mid = r1.astype(BF16)
    lo = (r1 - mid.astype(F32)).astype(BF16)
    return hi, mid, lo


def _rmsnorm_kernel(x_ref, g_ref, o_ref):
    x = x_ref[...]
    y = x * lax.rsqrt(jnp.mean(x * x, axis=-1, keepdims=True) + EPS)
    o_ref[...] = (y * g_ref[...]).astype(o_ref.dtype)


def _rmsnorm(x, g):
    t, d = x.shape
    tm = _pick(t, (512, 256, 128))
    return pl.pallas_call(
        _rmsnorm_kernel,
        grid=(t // tm,),
        in_specs=[pl.BlockSpec((tm, d), lambda i: (i, 0)), pl.BlockSpec((1, d), lambda i: (0, 0))],
        out_specs=pl.BlockSpec((tm, d), lambda i: (i, 0)),
        out_shape=jax.ShapeDtypeStruct((t, d), BF16),
        compiler_params=_params("parallel"),
        name="rmsnorm",
    )(x, g.reshape(1, d))


def _mm_kernel(a_ref, w_ref, *o_refs, scale, w_rows):
    if w_rows:
        r = lax.dot_general(a_ref[...], w_ref[...], (((1,), (1,)), ((), ())), preferred_element_type=F32)
    else:
        r = jnp.dot(a_ref[...], w_ref[...], preferred_element_type=F32)
    if scale is not None:
        r = r * scale
    for o_ref in o_refs:
        o_ref[...] = r.astype(o_ref.dtype)


def _mm(a, w, out_dtypes, col0=0, ncols=None, scale=None, w_rows=False):
    t, k = a.shape
    n = w.shape[0 if w_rows else 1] - col0 if ncols is None else ncols
    tm = _pick(t, (512, 256, 128))
    tn = _pick(n, (1024, 512, 256, 128))
    assert col0 % tn == 0
    j0 = col0 // tn
    if w_rows:
        w_spec = pl.BlockSpec((tn, k), lambda j, i: (j0 + j, 0))
    else:
        w_spec = pl.BlockSpec((k, tn), lambda j, i: (0, j0 + j))
    outs = pl.pallas_call(
        functools.partial(_mm_kernel, scale=scale, w_rows=w_rows),
        grid=(n // tn, t // tm),
        in_specs=[pl.BlockSpec((tm, k), lambda j, i: (i, 0)), w_spec],
        out_specs=[pl.BlockSpec((tm, tn), lambda j, i: (i, j)) for _ in out_dtypes],
        out_shape=[jax.ShapeDtypeStruct((t, n), dt) for dt in out_dtypes],
        compiler_params=_params("parallel", "parallel"),
        name="proj",
    )(a, w)
    return outs


def _norm_mm_kernel(x_ref, g_ref, w_ref, n_ref, o_ref, *, scale):
    x = x_ref[...]
    n = (x * lax.rsqrt(jnp.mean(x * x, axis=-1, keepdims=True) + EPS) * g_ref[...]).astype(BF16)
    n_ref[...] = n
    r = lax.dot_general(n, w_ref[...], (((1,), (1,)), ((), ())), preferred_element_type=F32)
    o_ref[...] = (r * scale).astype(o_ref.dtype)


def _norm_mm(x, g, w_t, ncols, scale):
    t, k = x.shape
    tm = _pick(t, (512, 256, 128))
    return pl.pallas_call(
        functools.partial(_norm_mm_kernel, scale=scale),
        grid=(t // tm,),
        in_specs=[pl.BlockSpec((tm, k), lambda i: (i, 0)), pl.BlockSpec((1, k), lambda i: (0, 0)),
                  pl.BlockSpec((ncols, k), lambda i: (0, 0), pipeline_mode=pl.Buffered(1))],
        out_specs=[pl.BlockSpec((tm, k), lambda i: (i, 0)), pl.BlockSpec((tm, ncols), lambda i: (i, 0))],
        out_shape=[jax.ShapeDtypeStruct((t, k), BF16), jax.ShapeDtypeStruct((t, ncols), BF16)],
        compiler_params=_params("parallel"),
        name="norm_proj",
    )(x, g.reshape(1, k), w_t)


def _small_kernel(n_ref, w_ref, b_ref, o_ref):
    z = lax.dot_general(n_ref[...], w_ref[...], (((1,), (1,)), ((), ())), preferred_element_type=F32)
    col = lax.broadcasted_iota(jnp.int32, z.shape, 1)
    o_ref[...] = jnp.where(col < FOX_HEADS, _log_sigmoid(z + b_ref[...]), z)


def _small_proj(n, w_small_t, b_fgate_pad):
    t, k = n.shape
    tm = _pick(t, (512, 256, 128))
    return pl.pallas_call(
        _small_kernel,
        grid=(t // tm,),
        in_specs=[pl.BlockSpec((tm, k), lambda i: (i, 0)), pl.BlockSpec((LANES, k), lambda i: (0, 0)),
                  pl.BlockSpec((1, LANES), lambda i: (0, 0))],
        out_specs=pl.BlockSpec((tm, LANES), lambda i: (i, 0)),
        out_shape=jax.ShapeDtypeStruct((t, LANES), F32),
        compiler_params=_params("parallel"),
        name="small_proj",
    )(n, w_small_t, b_fgate_pad)


def _cumsum_kernel(x_ref, o_ref, carry_ref, *, lb):
    @pl.when(pl.program_id(1) == 0)
    def _():
        carry_ref[...] = jnp.zeros_like(carry_ref)

    xt = x_ref[0].T
    r = lax.broadcasted_iota(jnp.int32, (lb, lb), 0)
    c = lax.broadcasted_iota(jnp.int32, (lb, lb), 1)
    tri = (r <= c).astype(BF16)
    acc = carry_ref[...]
    for part in _split3(xt):
        acc = acc + jnp.dot(part, tri, preferred_element_type=F32)
    o_ref[0] = acc
    carry_ref[...] = jnp.broadcast_to(acc[:, lb - 1:lb], carry_ref.shape)


def _cumsum_t(x, lb):
    b, l, _ = x.shape
    return pl.pallas_call(
        functools.partial(_cumsum_kernel, lb=lb),
        grid=(b, l // lb),
        in_specs=[pl.BlockSpec((1, lb, LANES), lambda i, j: (i, j, 0))],
        out_specs=pl.BlockSpec((1, LANES, lb), lambda i, j: (i, 0, j)),
        out_shape=jax.ShapeDtypeStruct((b, LANES, l), F32),
        scratch_shapes=[pltpu.VMEM((LANES, lb), F32)],
        compiler_params=_params("parallel", "arbitrary"),
        name="cumsum_logf",
    )(x)


def _fox_prompt_kernel(q_ref, k_ref, v_ref, c_ref, o_ref, m_sc, acc_sc, *, tq, hb):
    qi = pl.program_id(2)
    m_sc[...] = jnp.full_like(m_sc, NEG_INF)
    acc_sc[...] = jnp.zeros_like(acc_sc)

    def chunk(kc, masked):
        ks = pl.multiple_of(kc * tq, tq)
        ones = jnp.ones((tq, LANES), BF16)
        for hh in range(hb):
            cs = slice(hh * FOX_HEAD_DIM, (hh + 1) * FOX_HEAD_DIM)
            s = lax.dot_general(q_ref[:, cs], k_ref[pl.ds(ks, tq), cs], (((1,), (1,)), ((), ())),
                                preferred_element_type=F32) - c_ref[hh, kc] * LOG2E
            if masked:
                r = lax.broadcasted_iota(jnp.int32, s.shape, 0)
                c = lax.broadcasted_iota(jnp.int32, s.shape, 1)
                s = jnp.where(r >= c, s, NEG_INF)
            m_prev = m_sc[hh]
            m_new = jnp.maximum(m_prev, jnp.max(s, axis=1, keepdims=True))
            alpha = jnp.exp2(m_prev - m_new)
            p = jnp.exp2(s - jnp.concatenate([m_new] * (tq // LANES), axis=1))
            v_ext = jnp.concatenate([v_ref[pl.ds(ks, tq), cs], ones], axis=1)
            pv = jnp.dot(p.astype(BF16), v_ext, preferred_element_type=F32)
            acc_sc[hh] = jnp.concatenate([alpha, alpha], axis=1) * acc_sc[hh] + pv
            m_sc[hh] = m_new

    def body(kc, carry):
        chunk(kc, False)
        return carry

    lax.fori_loop(0, qi, body, 0)
    chunk(qi, True)
    for hh in range(hb):
        acc = acc_sc[hh]
        o_ref[:, hh * FOX_HEAD_DIM:(hh + 1) * FOX_HEAD_DIM] = (
            acc[:, :FOX_HEAD_DIM] / acc[:, FOX_HEAD_DIM:]).astype(o_ref.dtype)


def _fox_prompt(q16, k16, v16, c4, batch, seq):
    tq = _pick(seq, (512, 256, 128))
    nq = seq // tq
    hb = FOX_HEADS_PER_STEP
    wid = hb * FOX_HEAD_DIM
    return pl.pallas_call(
        functools.partial(_fox_prompt_kernel, tq=tq, hb=hb),
        grid=(batch, FOX_HEADS // hb, nq),
        in_specs=[
            pl.BlockSpec((tq, wid), lambda b, h, i: (b * nq + i, h)),
            pl.BlockSpec((seq, wid), lambda b, h, i: (b, h)),
            pl.BlockSpec((seq, wid), lambda b, h, i: (b, h)),
            pl.BlockSpec((hb, nq, 1, tq), lambda b, h, i: (b * (LANES // hb) + h, 0, 0, 0)),
        ],
        out_specs=pl.BlockSpec((tq, wid), lambda b, h, i: (b * nq + i, h)),
        out_shape=jax.ShapeDtypeStruct((batch * seq, FOX_WIDTH), BF16),
        scratch_shapes=[pltpu.VMEM((hb, tq, LANES), F32), pltpu.VMEM((hb, tq, 2 * FOX_HEAD_DIM), F32)],
        compiler_params=_params("parallel", "parallel", "arbitrary"),
        name="fox_prompt",
    )(q16, k16, v16, c4)


def _fox_sample_kernel(q_ref, kn_ref, vn_ref, ck_ref, cv_ref, c_ref, o_ref, *, past, new):
    pad = jnp.zeros((LANES - new, FOX_HEAD_DIM), BF16)
    dn = (((1,), (1,)), ((), ()))
    r = lax.broadcasted_iota(jnp.int32, (new, LANES), 0)
    cc = lax.broadcasted_iota(jnp.int32, (new, LANES), 1)
    for hd in range(FOX_HEADS):
        cs = slice(hd * FOX_HEAD_DIM, (hd + 1) * FOX_HEAD_DIM)
        q = q_ref[:, cs]
        kn = jnp.concatenate([kn_ref[:, cs], pad], axis=0)
        vn = jnp.concatenate([vn_ref[:, cs], pad], axis=0)
        c = c_ref[hd] * LOG2E
        ck = ck_ref[0, pl.ds(hd, past, stride=FOX_HEADS), :].astype(BF16)
        cv = cv_ref[0, pl.ds(hd, past, stride=FOX_HEADS), :].astype(BF16)
        s1 = lax.dot_general(q, ck, dn, preferred_element_type=F32) - c[:, :past]
        s2 = lax.dot_general(q, kn, dn, preferred_element_type=F32) - c[:, past:]
        s2 = jnp.where(r >= cc, s2, NEG_INF)
        m = jnp.maximum(jnp.max(s1, axis=1, keepdims=True), jnp.max(s2, axis=1, keepdims=True))
        p1 = jnp.exp2(s1 - m)
        p2 = jnp.exp2(s2 - m)
        l = jnp.sum(p1, axis=1, keepdims=True) + jnp.sum(p2, axis=1, keepdims=True)
        o = (jnp.dot(p1.astype(BF16), cv, preferred_element_type=F32)
             + jnp.dot(p2.astype(BF16), vn, preferred_element_type=F32))
        o_ref[:, cs] = (o / l).astype(o_ref.dtype)


def _fox_sample(q16, k16, v16, ck, cv, c3, batch, new, past):
    cache = pl.BlockSpec((1, past * FOX_HEADS, FOX_HEAD_DIM), lambda b: (b, 0, 0))
    rows = pl.BlockSpec((new, FOX_WIDTH), lambda b: (b, 0))
    return pl.pallas_call(
        functools.partial(_fox_sample_kernel, past=past, new=new),
        grid=(batch,),
        in_specs=[rows, rows, rows, cache, cache,
                  pl.BlockSpec((FOX_HEADS, 1, past + LANES), lambda b: (b * (LANES // FOX_HEADS), 0, 0))],
        out_specs=rows,
        out_shape=jax.ShapeDtypeStruct((batch * new, FOX_WIDTH), BF16),
        compiler_params=_params("parallel"),
        name="fox_sample",
    )(q16, k16, v16, ck, cv, c3)


def _gla_kernel(q_ref, k_ref, v_ref, gg_ref, sm_ref, wa_ref, ba_ref, g_ref, s0_ref, go_ref, sout_ref, s_sc,
                *, chunk, valid, has_init, hb):
    ci = pl.program_id(2)

    @pl.when(ci == 0)
    def _():
        if has_init:
            s_sc[...] = s0_ref[0]
        else:
            s_sc[...] = jnp.zeros_like(s_sc)

    for hh in range(hb):
        ks = slice(hh * GLA_DK, (hh + 1) * GLA_DK)
        vs = slice(hh * GLA_DV, (hh + 1) * GLA_DV)
        _gla_chunk(q_ref.at[:, ks], k_ref.at[:, ks], v_ref.at[:, vs], gg_ref.at[:, vs], sm_ref,
                   wa_ref.at[:, ks], ba_ref.at[:, ks], g_ref, s_sc.at[hh], go_ref.at[:, vs], chunk, valid)

    @pl.when(ci == pl.num_programs(2) - 1)
    def _():
        sout_ref[0] = s_sc[...]


def _gla_chunk(q_ref, k_ref, v_ref, gg_ref, sm_ref, wa_ref, ba_ref, g_ref, s_ref, go_ref, chunk, valid):
    C = chunk

    def rows(ref, dtype):
        x = ref[...].astype(dtype)
        if valid < C:
            x = jnp.concatenate([x, jnp.zeros((C - valid, x.shape[1]), dtype)], axis=0)
        return x

    row = lax.broadcasted_iota(jnp.int32, (C, C), 0)
    col = lax.broadcasted_iota(jnp.int32, (C, C), 1)

    a = jnp.dot(rows(sm_ref, BF16), wa_ref[...], preferred_element_type=F32) + ba_ref[...]
    lg = _log_sigmoid(a) * (1.0 / GLA_TAU)
    if valid < C:
        lg = jnp.where(lax.broadcasted_iota(jnp.int32, lg.shape, 0) < valid, lg, 0.0)
    low = (row >= col).astype(BF16)
    bc = jnp.zeros_like(lg)
    for part in _split3(lg):
        bc = bc + jnp.dot(low, part, preferred_element_type=F32)

    q = rows(q_ref, F32) * (GLA_DK ** -0.5)
    k = rows(k_ref, F32)
    v = rows(v_ref, BF16)
    s = s_ref[...]

    o_state = jnp.dot((q * jnp.exp(bc)).astype(BF16), s.astype(BF16), preferred_element_type=F32)
    krow = lax.broadcasted_iota(jnp.int32, (C, GLA_DK), 0)
    sub_row = lax.broadcasted_iota(jnp.int32, (GLA_SUB, C), 0)
    sub_col = lax.broadcasted_iota(jnp.int32, (GLA_SUB, C), 1)
    outs = []
    for i in range(C // GLA_SUB):
        r0, r1 = i * GLA_SUB, (i + 1) * GLA_SUB
        if valid <= r0:
            outs.append(o_state[r0:r1])
            continue
        base = bc[r0 - 1:r0] if i > 0 else jnp.zeros((1, GLA_DK), F32)
        qt = (q[r0:r1] * jnp.exp(bc[r0:r1] - base)).astype(BF16)
        kt = jnp.where(krow < r1, k * jnp.exp(base - bc), 0.0).astype(BF16)
        att = lax.dot_general(qt, kt, (((1,), (1,)), ((), ())), preferred_element_type=F32)
        att = jnp.where(sub_col <= sub_row + r0, att, 0.0)
        outs.append(o_state[r0:r1] + jnp.dot(att.astype(BF16), v, preferred_element_type=F32))
    o = jnp.concatenate(outs, axis=0)

    b_end = bc[C - 1:C]
    kd = k * jnp.exp(b_end - bc)
    decay = jnp.broadcast_to(jnp.exp(b_end), (LANES, GLA_DK)).T
    decay = jnp.concatenate([decay] * (GLA_DV // LANES), axis=1)
    s_new = decay * s + jnp.dot(kd.T.astype(BF16), v, preferred_element_type=F32)
    s_ref[...] = s_new

    o = o[:valid]
    on = o * lax.rsqrt(jnp.mean(o * o, axis=-1, keepdims=True) + EPS) * g_ref[...]
    gg = gg_ref[...]
    go_ref[...] = (on * (gg * jax.nn.sigmoid(gg))).astype(go_ref.dtype)


def _gla(z32, z16, small, wa_pad, b_a, g_out, s0, batch, tokens):
    if tokens >= GLA_CHUNKS[-1]:
        chunk = _pick(tokens, GLA_CHUNKS)
        blk, nch = chunk, tokens // chunk
    else:
        chunk, blk, nch = GLA_CHUNKS[-1], tokens, 1
    hb = GLA_HEADS_PER_STEP
    kw, vw = hb * GLA_DK, hb * GLA_DV
    has_init = s0 is not None
    if s0 is None:
        s0 = jnp.zeros((1, hb, GLA_DK, GLA_DV), F32)
        s0_map = lambda b, h, c: (0, 0, 0, 0)
    else:
        s0_map = lambda b, h, c: (b, h, 0, 0)
    qb, kb = COL_GQ // kw, COL_GK // kw
    vb, gb = COL_GV // vw, COL_GG // vw
    go, s_out = pl.pallas_call(
        functools.partial(_gla_kernel, chunk=chunk, valid=blk, has_init=has_init, hb=hb),
        grid=(batch, GLA_HEADS // hb, nch),
        in_specs=[
            pl.BlockSpec((blk, kw), lambda b, h, c: (b * nch + c, qb + h)),
            pl.BlockSpec((blk, kw), lambda b, h, c: (b * nch + c, kb + h)),
            pl.BlockSpec((blk, vw), lambda b, h, c: (b * nch + c, vb + h)),
            pl.BlockSpec((blk, vw), lambda b, h, c: (b * nch + c, gb + h)),
            pl.BlockSpec((blk, LANES), lambda b, h, c: (b * nch + c, 0)),
            pl.BlockSpec((LANES, kw), lambda b, h, c: (0, h)),
            pl.BlockSpec((1, kw), lambda b, h, c: (0, h)),
            pl.BlockSpec((1, GLA_DV), lambda b, h, c: (0, 0)),
            pl.BlockSpec((1, hb, GLA_DK, GLA_DV), s0_map),
        ],
        out_specs=[
            pl.BlockSpec((blk, vw), lambda b, h, c: (b * nch + c, h)),
            pl.BlockSpec((1, hb, GLA_DK, GLA_DV), lambda b, h, c: (b, h, 0, 0)),
        ],
        out_shape=[jax.ShapeDtypeStruct((batch * tokens, GLA_WIDTH), BF16),
                   jax.ShapeDtypeStruct((batch, GLA_HEADS, GLA_DK, GLA_DV), F32)],
        scratch_shapes=[pltpu.VMEM((hb, GLA_DK, GLA_DV), F32)],
        compiler_params=_params("parallel", "parallel", "arbitrary"),
        name="gla",
    )(z32, z32, z16, z32, small, wa_pad, b_a, g_out, s0)
    return go, s_out


def _outproj_kernel(fo_ref, go_ref, wt_ref, wb_ref, x_ref, o_ref):
    o_ref[...] = (x_ref[...] + jnp.dot(fo_ref[...], wt_ref[...], preferred_element_type=F32)
                  + jnp.dot(go_ref[...], wb_ref[...], preferred_element_type=F32))


def _outproj(fo, go, w_out, x):
    t, d = x.shape
    half = fo.shape[1]
    tm = _pick(t, (512, 256, 128))
    tn = _pick(d, (1024, 512))
    return pl.pallas_call(
        _outproj_kernel,
        grid=(d // tn, t // tm),
        in_specs=[
            pl.BlockSpec((tm, half), lambda j, i: (i, 0)),
            pl.BlockSpec((tm, half), lambda j, i: (i, 0)),
            pl.BlockSpec((half, tn), lambda j, i: (0, j)),
            pl.BlockSpec((half, tn), lambda j, i: (1, j)),
            pl.BlockSpec((tm, tn), lambda j, i: (i, j)),
        ],
        out_specs=pl.BlockSpec((tm, tn), lambda j, i: (i, j)),
        out_shape=jax.ShapeDtypeStruct((t, d), F32),
        compiler_params=_params("parallel", "parallel"),
        name="outproj",
    )(fo, go, w_out, w_out, x)


def _mem_kernel(h_ref, gq_ref, wq_ref, mk_ref, mv_ref, wo_ref, gf_ref, h2_ref, n3t_ref, *, per_batch, n_mem):
    h = h_ref[...]
    tm = h.shape[0]
    n2 = (h * lax.rsqrt(jnp.mean(h * h, axis=-1, keepdims=True) + EPS) * gq_ref[...]).astype(BF16)
    q = jnp.dot(n2, wq_ref[...], preferred_element_type=F32)
    scale = MEM_HEAD_DIM ** -0.5
    rows_per = min(per_batch, tm)
    parts = []
    for b in range(tm // rows_per):
        heads = []
        for hd in range(MEM_HEADS):
            cs = slice(hd * MEM_HEAD_DIM, (hd + 1) * MEM_HEAD_DIM)
            qh = q[b * rows_per:(b + 1) * rows_per, cs].astype(BF16)
            kh = mk_ref[b * n_mem:(b + 1) * n_mem, cs].astype(BF16)
            vh = mv_ref[b * n_mem:(b + 1) * n_mem, cs].astype(BF16)
            s = lax.dot_general(qh, kh, (((1,), (1,)), ((), ())), preferred_element_type=F32) * scale
            s = s - jnp.max(s, axis=1, keepdims=True)
            p = jnp.exp(s)
            p = p / jnp.sum(p, axis=1, keepdims=True)
            heads.append(jnp.dot(p.astype(BF16), vh, preferred_element_type=F32))
        parts.append(jnp.concatenate(heads, axis=1))
    o = jnp.concatenate(parts, axis=0).astype(BF16)
    h2 = h + jnp.dot(o, wo_ref[...], preferred_element_type=F32)
    h2_ref[...] = h2
    n3 = h2 * lax.rsqrt(jnp.mean(h2 * h2, axis=-1, keepdims=True) + EPS) * gf_ref[...]
    n3t_ref[...] = n3.T.astype(BF16)


def _mem_block(h1, g_q, w_mq, mk, mv, w_mo, g_ffn, per_batch, n_mem):
    t, d = h1.shape
    tm = _pick(t, (256, 128)) if per_batch >= 256 else LANES
    nb = max(tm // per_batch, 1)
    tiles_per_batch = max(per_batch // tm, 1)
    return pl.pallas_call(
        functools.partial(_mem_kernel, per_batch=per_batch, n_mem=n_mem),
        grid=(t // tm,),
        in_specs=[
            pl.BlockSpec((tm, d), lambda i: (i, 0)),
            pl.BlockSpec((1, d), lambda i: (0, 0)),
            pl.BlockSpec((d, MEM_WIDTH), lambda i: (0, 0)),
            pl.BlockSpec((nb * n_mem, MEM_WIDTH), lambda i: (i // tiles_per_batch, 0)),
            pl.BlockSpec((nb * n_mem, MEM_WIDTH), lambda i: (i // tiles_per_batch, 0)),
            pl.BlockSpec((MEM_WIDTH, d), lambda i: (0, 0)),
            pl.BlockSpec((1, d), lambda i: (0, 0)),
        ],
        out_specs=[pl.BlockSpec((tm, d), lambda i: (i, 0)), pl.BlockSpec((d, tm), lambda i: (0, i))],
        out_shape=[jax.ShapeDtypeStruct((t, d), F32), jax.ShapeDtypeStruct((d, t), BF16)],
        compiler_params=_params("parallel"),
        name="mem_attn",
    )(h1, g_q, w_mq, mk, mv, w_mo, g_ffn)


def _extract_top(vals, key, count, exact):
    cur = vals
    rank = jnp.full(vals.shape, float(count), F32)
    tops = []
    big = jnp.float32(2**30)
    for r in range(count):
        m = jnp.max(cur, axis=0, keepdims=True)
        hit = cur == m
        if exact:
            first = jnp.min(jnp.where(hit, key, big), axis=0, keepdims=True)
            hit = key == first
        rank = jnp.where(hit, float(r), rank)
        cur = jnp.where(hit, -jnp.inf, cur)
        tops.append(m)
    return tops, rank


def _peer_route_kernel(w_ref, n3t_ref, keys_ref, pk2_ref, pk1_ref, qp_sc):
    qp_sc[...] = jnp.dot(w_ref[...], n3t_ref[...], preferred_element_type=F32)
    qd = 2 * PEER_HALF

    def head(hd, carry):
        r0 = pl.multiple_of(hd * qd, qd)
        s1 = jnp.dot(keys_ref[0], qp_sc[pl.ds(r0, PEER_HALF), :].astype(BF16), preferred_element_type=F32)
        s2 = jnp.dot(keys_ref[1], qp_sc[pl.ds(r0 + PEER_HALF, PEER_HALF), :].astype(BF16),
                     preferred_element_type=F32)

        def store(tables):
            rank2, e2, cnt, e1 = tables
            pk2_ref[hd, 0] = rank2.astype(BF16)
            pk2_ref[hd, 1] = e2.astype(BF16)
            pk1_ref[hd, 0] = cnt
            pk1_ref[hd, 1] = e1

        tables, removed = _route_tables(s1, s2, exact=False)
        store(tables)
        ties = jnp.max(jnp.abs(removed - float(3 * PEER_TOPK)))

        @pl.when(ties > 0.0)
        def _():
            store(_route_tables(s1, s2, exact=True)[0])

        return carry

    lax.fori_loop(0, PEER_HEADS, head, 0)


def _route_tables(s1, s2, exact):
    K = PEER_TOPK
    tm = s1.shape[1]
    kiota = lax.broadcasted_iota(jnp.int32, s1.shape, 0).astype(F32)
    a, rank1 = _extract_top(s1, kiota, K, exact)
    b, rank2 = _extract_top(s2, kiota, K, exact)

    half = K // 2
    bmat = jnp.concatenate(b, axis=0)
    jio = lax.broadcasted_iota(jnp.int32, (K, tm), 0).astype(F32)
    jio_h = lax.broadcasted_iota(jnp.int32, (half, tm), 0).astype(F32)
    blocks, keys, spans = [a[0] + bmat], [jio], [(0, K)]
    for i in range(1, half):
        blocks.append(jnp.where(jio_h < float(K // (i + 1)), a[i] + bmat[:half], -jnp.inf))
        keys.append(jio_h + float(i * K))
        spans.append((K + (i - 1) * half, K + i * half))
    blocks.append(jnp.concatenate([a[i] + b[0] for i in range(half, K)], axis=0))
    keys.append((jio_h + float(half)) * float(K))
    tail = K + (half - 1) * half
    spans += [(tail + i, tail + i + 1) for i in range(K - half)]
    cand = jnp.concatenate(blocks, axis=0)
    ckey = jnp.concatenate(keys, axis=0)
    _, crank = _extract_top(cand, ckey, K, exact)
    sel = crank < float(K)
    z = jnp.sum(jnp.where(sel, jnp.exp(cand - (a[0] + b[0])), 0.0), axis=0, keepdims=True)
    self32 = sel.astype(F32)
    counts = [jnp.sum(self32[lo:hi], axis=0, keepdims=True) for lo, hi in spans]

    cnt = jnp.zeros(s1.shape, F32)
    for i in range(K):
        cnt = jnp.where(rank1 == float(i), counts[i], cnt)
    removed = (jnp.sum((rank1 < float(K)).astype(F32), axis=0, keepdims=True)
               + jnp.sum((rank2 < float(K)).astype(F32), axis=0, keepdims=True)
               + jnp.sum(self32, axis=0, keepdims=True))
    return (rank2, jnp.exp(s2 - b[0]), cnt, jnp.exp(s1 - a[0]) / z), removed


def _peer_route(w_pq_t, n3t, keys):
    d, t = n3t.shape
    tm = _pick(t, (512, 256, 128))
    qw = PEER_HEADS * 2 * PEER_HALF
    spec = pl.BlockSpec((PEER_HEADS, 2, PEER_NKEYS, tm), lambda i: (0, 0, 0, i))
    return pl.pallas_call(
        _peer_route_kernel,
        grid=(t // tm,),
        in_specs=[
            pl.BlockSpec((qw, d), lambda i: (0, 0), pipeline_mode=pl.Buffered(1)),
            pl.BlockSpec((d, tm), lambda i: (0, i)),
            pl.BlockSpec((2, PEER_NKEYS, PEER_HALF), lambda i: (0, 0, 0)),
        ],
        out_specs=[spec, spec],
        out_shape=[jax.ShapeDtypeStruct((PEER_HEADS, 2, PEER_NKEYS, t), BF16),
                   jax.ShapeDtypeStruct((PEER_HEADS, 2, PEER_NKEYS, t), F32)],
        scratch_shapes=[pltpu.VMEM((qw, tm), F32)],
        compiler_params=_params("parallel"),
        name="peer_route",
    )(w_pq_t, n3t, keys)


def _gelu_tanh(x):
    inner = x * (0.7978845608028654 + (0.7978845608028654 * 0.044715) * (x * x))
    half = 0.5 * x
    return half + half * jnp.tanh(inner)


def _peer_kernel(n3t_ref, pk2_ref, pk1_ref, u_ref, vt_ref, o_ref, act_sc, *, ec, n_chunks):
    s = pl.program_id(0)
    cur = s % 2
    acc_pair = jnp.maximum(s - 1, 0)
    act_pair = jnp.minimum(s, pl.num_programs(0) - 2)

    @pl.when(s == 0)
    def _():
        act_sc[1] = jnp.zeros(act_sc.shape[1:], act_sc.dtype)

    @pl.when(acc_pair % n_chunks == 0)
    def _():
        o_ref[...] = jnp.zeros_like(o_ref)

    h = jnp.dot(u_ref[...], n3t_ref[...], preferred_element_type=F32)
    o_ref[...] += jnp.dot(vt_ref[0], act_sc[1 - cur], preferred_element_type=F32)
    k1_base = (act_pair % n_chunks) * (ec // PEER_NKEYS)
    for kk in range(ec // PEER_NKEYS):
        rows = slice(kk * PEER_NKEYS, (kk + 1) * PEER_NKEYS)
        gate = jnp.zeros((PEER_NKEYS, h.shape[1]), BF16)
        for hd in range(PEER_HEADS):
            cnt = pk1_ref[hd, 0, pl.ds(k1_base + kk, 1), :].astype(BF16)
            e1 = pk1_ref[hd, 1, pl.ds(k1_base + kk, 1), :].astype(BF16)
            gate = gate + jnp.where(pk2_ref[hd, 0] < cnt, pk2_ref[hd, 1], 0.0) * e1
        act_sc[cur, rows, :] = (_gelu_tanh(h[rows]) * gate.astype(F32)).astype(BF16)


def _peer(n3t, pk2, pk1, u, vt):
    d, t = n3t.shape
    n_chunks, _, ec = vt.shape
    tm = _pick(t, (512, 256, 128))
    once = pl.Buffered(1)
    n_pairs = (t // tm) * n_chunks
    act_pair = lambda s: jnp.minimum(s, n_pairs - 1)
    acc_pair = lambda s: jnp.maximum(s - 1, 0)
    return pl.pallas_call(
        functools.partial(_peer_kernel, ec=ec, n_chunks=n_chunks),
        grid=(n_pairs + 1,),
        in_specs=[
            pl.BlockSpec((d, tm), lambda s: (0, act_pair(s) // n_chunks), pipeline_mode=once),
            pl.BlockSpec((PEER_HEADS, 2, PEER_NKEYS, tm), lambda s: (0, 0, 0, act_pair(s) // n_chunks),
                         pipeline_mode=once),
            pl.BlockSpec((PEER_HEADS, 2, PEER_NKEYS, tm), lambda s: (0, 0, 0, act_pair(s) // n_chunks),
                         pipeline_mode=once),
            pl.BlockSpec((ec, d), lambda s: (act_pair(s) % n_chunks, 0)),
            pl.BlockSpec((1, d, ec), lambda s: (acc_pair(s) % n_chunks, 0, 0)),
        ],
        out_specs=pl.BlockSpec((d, tm), lambda s: (0, acc_pair(s) // n_chunks)),
        out_shape=jax.ShapeDtypeStruct((d, t), F32),
        scratch_shapes=[pltpu.VMEM((2, ec, tm), BF16)],
        compiler_params=_params("arbitrary"),
        name="peer_experts",
    )(n3t, pk2, pk1, u, vt)


def _final_kernel(h_ref, pt_ref, g_ref, o_ref):
    x = h_ref[...] + pt_ref[...].T
    y = x * lax.rsqrt(jnp.mean(x * x, axis=-1, keepdims=True) + EPS)
    o_ref[...] = y * g_ref[...]


def _final(h2, peer_t, g):
    t, d = h2.shape
    tm = _pick(t, (256, 128))
    return pl.pallas_call(
        _final_kernel,
        grid=(t // tm,),
        in_specs=[pl.BlockSpec((tm, d), lambda i: (i, 0)), pl.BlockSpec((d, tm), lambda i: (0, i)),
                  pl.BlockSpec((1, d), lambda i: (0, 0))],
        out_specs=pl.BlockSpec((tm, d), lambda i: (i, 0)),
        out_shape=jax.ShapeDtypeStruct((t, d), F32),
        compiler_params=_params("parallel"),
        name="final_norm",
    )(h2, peer_t, g)


def _layer(x, w, mem_k, mem_v, batch, tokens, n_mem, past):
    n, q16 = _norm_mm(x, w["norm_mix"], w["w_fox"], FOX_WIDTH, FOX_QSCALE)
    fk, k16 = _mm(n, w["w_fox"], (F32, BF16), COL_FK, FOX_WIDTH, w_rows=True)
    fv, v16 = _mm(n, w["w_fox"], (F32, BF16), COL_FV, FOX_WIDTH, w_rows=True)
    z32, z16 = _mm(n, w["w_gla"], (F32, BF16), w_rows=True)
    small = _small_proj(n, w["w_small"], w["b_fgate"])
    logf = small[:, :FOX_HEADS]

    if past is None:
        lb = _pick(tokens, (512, 256, 128))
        c = _cumsum_t(small.reshape(batch, tokens, LANES), lb)
        tq = _pick(tokens, (512, 256, 128))
        c4 = c.reshape(batch * LANES, tokens // tq, 1, tq)
        fo = _fox_prompt(q16, k16, v16, c4, batch, tokens)
        go, gla_state = _gla(z32, z16, small, w["wa_pad"], w["b_gla_a"], w["norm_gla_out"], None, batch, tokens)
    else:
        ck, cv, clogf, s0 = past
        plen = ck.shape[1]
        lf = jnp.concatenate([
            jnp.pad(clogf.astype(F32), ((0, 0), (0, 0), (0, LANES - FOX_HEADS))),
            small.reshape(batch, tokens, LANES),
            jnp.zeros((batch, LANES - tokens, LANES), F32)], axis=1)
        c = _cumsum_t(lf, LANES)
        c3 = c.reshape(batch * LANES, 1, plen + LANES)
        fo = _fox_sample(q16, k16, v16, ck.reshape(batch, plen * FOX_HEADS, FOX_HEAD_DIM),
                         cv.reshape(batch, plen * FOX_HEADS, FOX_HEAD_DIM), c3, batch, tokens, plen)
        go, gla_state = _gla(z32, z16, small, w["wa_pad"], w["b_gla_a"], w["norm_gla_out"], s0.astype(F32),
                             batch, tokens)

    h1 = _outproj(fo, go, w["w_out"], x)
    h2, n3t = _mem_block(h1, w["norm_mem_q"], w["w_mq"], mem_k, mem_v, w["w_mo"], w["norm_ffn"], tokens, n_mem)
    pk2, pk1 = _peer_route(w["w_pq_t"], n3t, w["sub_keys"])
    peer_t = _peer(n3t, pk2, pk1, w["expert_u"], w["expert_v_t"])
    return h2, peer_t, fk, fv, logf, gla_state


def kernel(x_prompt, x_sample, cache_fox_k, cache_fox_v, cache_fox_logf, state_gla, cache_mem_k, cache_mem_v, mem_prompt, norm_mix, w_in, b_fgate, w_gla_a2, b_gla_a, norm_gla_out, w_out, norm_mem_q, norm_mem_kv, w_mq, w_mk, w_mv, w_mo, norm_ffn, w_pq, sub_keys, expert_u, expert_v, norm_final):
    depth = w_in.shape[0]
    assert depth == 1, "one trunk layer"
    bp, sp, d = x_prompt.shape
    bs, ss, _ = x_sample.shape
    n_mem = mem_prompt.shape[1]
    l = 0

    wt = jnp.swapaxes(w_in, 1, 2)[l]
    o_ff = 3 * FOX_WIDTH
    o_g = o_ff + FOX_HEADS
    o_ga = o_g + GLA_PROJ_WIDTH
    w = {
        "norm_mix": norm_mix[l],
        "w_fox": wt[:o_ff].astype(BF16),
        "w_gla": wt[o_g:o_ga].astype(BF16),
        "w_small": jnp.concatenate([wt[o_ff:o_g], wt[o_ga:],
                                    jnp.zeros((LANES - FOX_HEADS - GLA_GATE_RANK, d), F32)], axis=0).astype(BF16),
        "b_fgate": jnp.pad(b_fgate[l], (0, LANES - FOX_HEADS)).reshape(1, LANES),
        "wa_pad": jnp.pad(w_gla_a2[l], ((FOX_HEADS, LANES - FOX_HEADS - GLA_GATE_RANK), (0, 0))).astype(BF16),
        "b_gla_a": b_gla_a[l].reshape(1, GLA_KEY_WIDTH),
        "norm_gla_out": norm_gla_out[l].reshape(1, GLA_DV),
        "w_out": w_out[l].astype(BF16),
        "norm_mem_q": norm_mem_q[l].reshape(1, d),
        "w_mq": w_mq[l].astype(BF16),
        "w_mo": w_mo[l].astype(BF16),
        "norm_ffn": norm_ffn[l].reshape(1, d),
        "w_pq_t": w_pq[l].T.astype(BF16),
        "sub_keys": sub_keys[l].astype(BF16),
        "expert_u": expert_u[l].astype(BF16),
        "expert_v_t": expert_v[l].reshape(-1, PEER_EXPERT_CHUNK, d).transpose(0, 2, 1).astype(BF16),
    }

    m = _rmsnorm(mem_prompt.reshape(bp * n_mem, d), norm_mem_kv[l])
    (mk,) = _mm(m, w_mk[l].astype(BF16), (F32,))
    (mv,) = _mm(m, w_mv[l].astype(BF16), (F32,))

    hp, pp, fkp, fvp, lfp, gsp = _layer(x_prompt.reshape(bp * sp, d), w, mk, mv, bp, sp, n_mem, None)
    past = (cache_fox_k[l], cache_fox_v[l], cache_fox_logf[l], state_gla[l])
    hs, ps, fks, fvs, lfs, gss = _layer(x_sample.reshape(bs * ss, d), w,
                                        cache_mem_k[l].reshape(bs * n_mem, MEM_WIDTH),
                                        cache_mem_v[l].reshape(bs * n_mem, MEM_WIDTH), bs, ss, n_mem, past)

    g_fin = norm_final.reshape(1, d)
    y_prompt = _final(hp, pp, g_fin).reshape(bp, sp, d)
    y_sample = _final(hs, ps, g_fin).reshape(bs, ss, d)
    hshape_p = (1, bp, sp, FOX_HEADS, FOX_HEAD_DIM)
    hshape_s = (1, bs, ss, FOX_HEADS, FOX_HEAD_DIM)
    return (y_prompt, y_sample,
            fkp.reshape(hshape_p), fvp.reshape(hshape_p), lfp.reshape(1, bp, sp, FOX_HEADS), gsp[None],
            mk.reshape(1, bp, n_mem, MEM_HEADS, MEM_HEAD_DIM), mv.reshape(1, bp, n_mem, MEM_HEADS, MEM_HEAD_DIM),
            fks.reshape(hshape_s), fvs.reshape(hshape_s), lfs.reshape(1, bs, ss, FOX_HEADS), gss[None])
```

```python
import functools

import jax
import jax.numpy as jnp
from jax import lax
from jax.experimental import pallas as pl
from jax.experimental.pallas import tpu as pltpu

F32 = jnp.float32
BF16 = jnp.bfloat16

EPS = 1e-6
NEG_INF = -1e30

FOX_HEADS = 16
FOX_HEAD_DIM = 128
FOX_WIDTH = FOX_HEADS * FOX_HEAD_DIM
FOX_HEADS_PER_STEP = 8
LOG2E = 1.4426950408889634
FOX_QSCALE = FOX_HEAD_DIM ** -0.5 * LOG2E
GLA_HEADS = 4
GLA_DK = 256
GLA_DV = 512
GLA_KEY_WIDTH = GLA_HEADS * GLA_DK
GLA_WIDTH = GLA_HEADS * GLA_DV
GLA_GATE_RANK = 16
GLA_TAU = 16.0
GLA_CHUNKS = (256, 128)
GLA_SUB = 32
GLA_HEADS_PER_STEP = 4
MEM_HEADS = 4
MEM_HEAD_DIM = 128
MEM_WIDTH = MEM_HEADS * MEM_HEAD_DIM
PEER_HEADS = 8
PEER_NKEYS = 128
PEER_HALF = 128
PEER_TOPK = 16
PEER_EXPERT_CHUNK = 512
LANES = 128
GLA_PROJ_WIDTH = 2 * GLA_KEY_WIDTH + 2 * GLA_WIDTH

COL_FQ, COL_FK, COL_FV = 0, FOX_WIDTH, 2 * FOX_WIDTH
COL_GQ = 0
COL_GK = COL_GQ + GLA_KEY_WIDTH
COL_GV = COL_GK + GLA_KEY_WIDTH
COL_GG = COL_GV + GLA_WIDTH

VMEM_LIMIT_BYTES = 52 * 2**20


def _pick(n, prefs):
    for p in prefs:
        if n % p == 0:
            return p
    raise ValueError(f"no tile in {prefs} divides {n}")


def _params(*sem):
    return pltpu.CompilerParams(dimension_semantics=sem, vmem_limit_bytes=VMEM_LIMIT_BYTES)


def _log_sigmoid(x):
    return -(jnp.maximum(-x, 0.0) + jnp.log1p(jnp.exp(-jnp.abs(x))))


def _split3(x):
    hi = x.astype(BF16)
    r1 = x - hi.astype(F32)
    mid = r1.astype(BF16)
    lo = (r1 - mid.astype(F32)).astype(BF16)
    return hi, mid, lo


def _rmsnorm_kernel(x_ref, g_ref, o_ref):
    x = x_ref[...]
    y = x * lax.rsqrt(jnp.mean(x * x, axis=-1, keepdims=True) + EPS)
    o_ref[...] = (y * g_ref[...]).astype(o_ref.dtype)


def _rmsnorm(x, g):
    t, d = x.shape
    tm = _pick(t, (512, 256, 128))
    return pl.pallas_call(
        _rmsnorm_kernel,
        grid=(t // tm,),
        in_specs=[pl.BlockSpec((tm, d), lambda i: (i, 0)), pl.BlockSpec((1, d), lambda i: (0, 0))],
        out_specs=pl.BlockSpec((tm, d), lambda i: (i, 0)),
        out_shape=jax.ShapeDtypeStruct((t, d), BF16),
        compiler_params=_params("parallel"),
        name="rmsnorm",
    )(x, g.reshape(1, d))


def _mm_kernel(a_ref, w_ref, *o_refs, scale, w_rows):
    if w_rows:
        r = lax.dot_general(a_ref[...], w_ref[...], (((1,), (1,)), ((), ())), preferred_element_type=F32)
    else:
        r = jnp.dot(a_ref[...], w_ref[...], preferred_element_type=F32)
    if scale is not None:
        r = r * scale
    for o_ref in o_refs:
        o_ref[...] = r.astype(o_ref.dtype)


def _mm(a, w, out_dtypes, col0=0, ncols=None, scale=None, w_rows=False):
    t, k = a.shape
    n = w.shape[0 if w_rows else 1] - col0 if ncols is None else ncols
    tm = _pick(t, (1024, 512, 256, 128))
    tn = _pick(n, (1024, 512, 256, 128))
    assert col0 % tn == 0
    j0 = col0 // tn
    if w_rows:
        w_spec = pl.BlockSpec((tn, k), lambda j, i: (j0 + j, 0))
    else:
        w_spec = pl.BlockSpec((k, tn), lambda j, i: (0, j0 + j))
    outs = pl.pallas_call(
        functools.partial(_mm_kernel, scale=scale, w_rows=w_rows),
        grid=(n // tn, t // tm),
        in_specs=[pl.BlockSpec((tm, k), lambda j, i: (i, 0)), w_spec],
        out_specs=[pl.BlockSpec((tm, tn), lambda j, i: (i, j)) for _ in out_dtypes],
        out_shape=[jax.ShapeDtypeStruct((t, n), dt) for dt in out_dtypes],
        compiler_params=_params("parallel", "parallel"),
        name="proj",
    )(a, w)
    return outs


def _norm_mm_kernel(x_ref, g_ref, w_ref, n_ref, o_ref, *, scale):
    x = x_ref[...]
    n = (x * lax.rsqrt(jnp.mean(x * x, axis=-1, keepdims=True) + EPS) * g_ref[...]).astype(BF16)
    n_ref[...] = n
    r = lax.dot_general(n, w_ref[...], (((1,), (1,)), ((), ())), preferred_element_type=F32)
    o_ref[...] = (r * scale).astype(o_ref.dtype)


def _norm_mm(x, g, w_t, ncols, scale):
    t, k = x.shape
    tm = _pick(t, (512, 256, 128))
    return pl.pallas_call(
        functools.partial(_norm_mm_kernel, scale=scale),
        grid=(t // tm,),
        in_specs=[pl.BlockSpec((tm, k), lambda i: (i, 0)), pl.BlockSpec((1, k), lambda i: (0, 0)),
                  pl.BlockSpec((ncols, k), lambda i: (0, 0), pipeline_mode=pl.Buffered(1))],
        out_specs=[pl.BlockSpec((tm, k), lambda i: (i, 0)), pl.BlockSpec((tm, ncols), lambda i: (i, 0))],
        out_shape=[jax.ShapeDtypeStruct((t, k), BF16), jax.ShapeDtypeStruct((t, ncols), BF16)],
        compiler_params=_params("parallel"),
        name="norm_proj",
    )(x, g.reshape(1, k), w_t)


def _small_kernel(n_ref, w_ref, b_ref, o_ref):
    z = lax.dot_general(n_ref[...], w_ref[...], (((1,), (1,)), ((), ())), preferred_element_type=F32)
    col = lax.broadcasted_iota(jnp.int32, z.shape, 1)
    o_ref[...] = jnp.where(col < FOX_HEADS, _log_sigmoid(z + b_ref[...]), z)


def _small_proj(n, w_small_t, b_fgate_pad):
    t, k = n.shape
    tm = _pick(t, (512, 256, 128))
    return pl.pallas_call(
        _small_kernel,
        grid=(t // tm,),
        in_specs=[pl.BlockSpec((tm, k), lambda i: (i, 0)), pl.BlockSpec((LANES, k), lambda i: (0, 0)),
                  pl.BlockSpec((1, LANES), lambda i: (0, 0))],
        out_specs=pl.BlockSpec((tm, LANES), lambda i: (i, 0)),
        out_shape=jax.ShapeDtypeStruct((t, LANES), F32),
        compiler_params=_params("parallel"),
        name="small_proj",
    )(n, w_small_t, b_fgate_pad)


def _cumsum_kernel(x_ref, o_ref, carry_ref, *, lb):
    @pl.when(pl.program_id(1) == 0)
    def _():
        carry_ref[...] = jnp.zeros_like(carry_ref)

    xt = x_ref[0].T
    r = lax.broadcasted_iota(jnp.int32, (lb, lb), 0)
    c = lax.broadcasted_iota(jnp.int32, (lb, lb), 1)
    tri = (r <= c).astype(BF16)
    acc = carry_ref[...]
    for part in _split3(xt):
        acc = acc + jnp.dot(part, tri, preferred_element_type=F32)
    o_ref[0] = acc
    carry_ref[...] = jnp.broadcast_to(acc[:, lb - 1:lb], carry_ref.shape)


def _cumsum_t(x, lb):
    b, l, _ = x.shape
    return pl.pallas_call(
        functools.partial(_cumsum_kernel, lb=lb),
        grid=(b, l // lb),
        in_specs=[pl.BlockSpec((1, lb, LANES), lambda i, j: (i, j, 0))],
        out_specs=pl.BlockSpec((1, LANES, lb), lambda i, j: (i, 0, j)),
        out_shape=jax.ShapeDtypeStruct((b, LANES, l), F32),
        scratch_shapes=[pltpu.VMEM((LANES, lb), F32)],
        compiler_params=_params("parallel", "arbitrary"),
        name="cumsum_logf",
    )(x)


def _fox_prompt_kernel(q_ref, k_ref, v_ref, c_ref, o_ref, m_sc, acc_sc, *, tq, hb):
    qi = pl.program_id(2)
    m_sc[...] = jnp.full_like(m_sc, NEG_INF)
    acc_sc[...] = jnp.zeros_like(acc_sc)

    def chunk(kc, masked):
        ks = pl.multiple_of(kc * tq, tq)
        ones = jnp.ones((tq, LANES), BF16)
        for hh in range(hb):
            cs = slice(hh * FOX_HEAD_DIM, (hh + 1) * FOX_HEAD_DIM)
            s = lax.dot_general(q_ref[:, cs], k_ref[pl.ds(ks, tq), cs], (((1,), (1,)), ((), ())),
                                preferred_element_type=F32) - c_ref[hh, kc] * LOG2E
            if masked:
                r = lax.broadcasted_iota(jnp.int32, s.shape, 0)
                c = lax.broadcasted_iota(jnp.int32, s.shape, 1)
                s = jnp.where(r >= c, s, NEG_INF)
            m_prev = m_sc[hh]
            m_new = jnp.maximum(m_prev, jnp.max(s, axis=1, keepdims=True))
            alpha = jnp.exp2(m_prev - m_new)
            p = jnp.exp2(s - jnp.concatenate([m_new] * (tq // LANES), axis=1))
            v_ext = jnp.concatenate([v_ref[pl.ds(ks, tq), cs], ones], axis=1)
            pv = jnp.dot(p.astype(BF16), v_ext, preferred_element_type=F32)
            acc_sc[hh] = jnp.concatenate([alpha, alpha], axis=1) * acc_sc[hh] + pv
            m_sc[hh] = m_new

    def body(kc, carry):
        chunk(kc, False)
        return carry

    lax.fori_loop(0, qi, body, 0)
    chunk(qi, True)
    for hh in range(hb):
        acc = acc_sc[hh]
        o_ref[:, hh * FOX_HEAD_DIM:(hh + 1) * FOX_HEAD_DIM] = (
            acc[:, :FOX_HEAD_DIM] / acc[:, FOX_HEAD_DIM:]).astype(o_ref.dtype)


def _fox_prompt(q16, k16, v16, c4, batch, seq):
    tq = _pick(seq, (512, 256, 128))
    nq = seq // tq
    hb = FOX_HEADS_PER_STEP
    wid = hb * FOX_HEAD_DIM
    return pl.pallas_call(
        functools.partial(_fox_prompt_kernel, tq=tq, hb=hb),
        grid=(batch, FOX_HEADS // hb, nq),
        in_specs=[
            pl.BlockSpec((tq, wid), lambda b, h, i: (b * nq + i, h)),
            pl.BlockSpec((seq, wid), lambda b, h, i: (b, h)),
            pl.BlockSpec((seq, wid), lambda b, h, i: (b, h)),
            pl.BlockSpec((hb, nq, 1, tq), lambda b, h, i: (b * (LANES // hb) + h, 0, 0, 0)),
        ],
        out_specs=pl.BlockSpec((tq, wid), lambda b, h, i: (b * nq + i, h)),
        out_shape=jax.ShapeDtypeStruct((batch * seq, FOX_WIDTH), BF16),
        scratch_shapes=[pltpu.VMEM((hb, tq, LANES), F32), pltpu.VMEM((hb, tq, 2 * FOX_HEAD_DIM), F32)],
        compiler_params=_params("parallel", "parallel", "arbitrary"),
        name="fox_prompt",
    )(q16, k16, v16, c4)


def _fox_sample_kernel(q_ref, kn_ref, vn_ref, ck_ref, cv_ref, c_ref, o_ref, *, past, new):
    pad = jnp.zeros((LANES - new, FOX_HEAD_DIM), BF16)
    dn = (((1,), (1,)), ((), ()))
    r = lax.broadcasted_iota(jnp.int32, (new, LANES), 0)
    cc = lax.broadcasted_iota(jnp.int32, (new, LANES), 1)
    for hd in range(FOX_HEADS):
        cs = slice(hd * FOX_HEAD_DIM, (hd + 1) * FOX_HEAD_DIM)
        q = q_ref[:, cs]
        kn = jnp.concatenate([kn_ref[:, cs], pad], axis=0)
        vn = jnp.concatenate([vn_ref[:, cs], pad], axis=0)
        c = c_ref[hd] * LOG2E
        ck = ck_ref[0, pl.ds(hd, past, stride=FOX_HEADS), :].astype(BF16)
        cv = cv_ref[0, pl.ds(hd, past, stride=FOX_HEADS), :].astype(BF16)
        s1 = lax.dot_general(q, ck, dn, preferred_element_type=F32) - c[:, :past]
        s2 = lax.dot_general(q, kn, dn, preferred_element_type=F32) - c[:, past:]
        s2 = jnp.where(r >= cc, s2, NEG_INF)
        m = jnp.maximum(jnp.max(s1, axis=1, keepdims=True), jnp.max(s2, axis=1, keepdims=True))
        p1 = jnp.exp2(s1 - m)
        p2 = jnp.exp2(s2 - m)
        l = jnp.sum(p1, axis=1, keepdims=True) + jnp.sum(p2, axis=1, keepdims=True)
        o = (jnp.dot(p1.astype(BF16), cv, preferred_element_type=F32)
             + jnp.dot(p2.astype(BF16), vn, preferred_element_type=F32))
        o_ref[:, cs] = (o / l).astype(o_ref.dtype)


def _fox_sample(q16, k16, v16, ck, cv, c3, batch, new, past):
    cache = pl.BlockSpec((1, past * FOX_HEADS, FOX_HEAD_DIM), lambda b: (b, 0, 0))
    rows = pl.BlockSpec((new, FOX_WIDTH), lambda b: (b, 0))
    return pl.pallas_call(
        functools.partial(_fox_sample_kernel, past=past, new=new),
        grid=(batch,),
        in_specs=[rows, rows, rows, cache, cache,
                  pl.BlockSpec((FOX_HEADS, 1, past + LANES), lambda b: (b * (LANES // FOX_HEADS), 0, 0))],
        out_specs=rows,
        out_shape=jax.ShapeDtypeStruct((batch * new, FOX_WIDTH), BF16),
        compiler_params=_params("parallel"),
        name="fox_sample",
    )(q16, k16, v16, ck, cv, c3)


def _gla_kernel(q_ref, k_ref, v_ref, gg_ref, sm_ref, wa_ref, ba_ref, g_ref, s0_ref, go_ref, sout_ref, s_sc,
                *, chunk, valid, has_init, hb):
    ci = pl.program_id(2)

    @pl.when(ci == 0)
    def _():
        if has_init:
            s_sc[...] = s0_ref[0]
        else:
            s_sc[...] = jnp.zeros_like(s_sc)

    for hh in range(hb):
        ks = slice(hh * GLA_DK, (hh + 1) * GLA_DK)
        vs = slice(hh * GLA_DV, (hh + 1) * GLA_DV)
        _gla_chunk(q_ref.at[:, ks], k_ref.at[:, ks], v_ref.at[:, vs], gg_ref.at[:, vs], sm_ref,
                   wa_ref.at[:, ks], ba_ref.at[:, ks], g_ref, s_sc.at[hh], go_ref.at[:, vs], chunk, valid)

    @pl.when(ci == pl.num_programs(2) - 1)
    def _():
        sout_ref[0] = s_sc[...]


def _gla_chunk(q_ref, k_ref, v_ref, gg_ref, sm_ref, wa_ref, ba_ref, g_ref, s_ref, go_ref, chunk, valid):
    C = chunk

    def rows(ref, dtype):
        x = ref[...].astype(dtype)
        if valid < C:
            x = jnp.concatenate([x, jnp.zeros((C - valid, x.shape[1]), dtype)], axis=0)
        return x

    row = lax.broadcasted_iota(jnp.int32, (C, C), 0)
    col = lax.broadcasted_iota(jnp.int32, (C, C), 1)

    a = jnp.dot(rows(sm_ref, BF16), wa_ref[...], preferred_element_type=F32) + ba_ref[...]
    lg = _log_sigmoid(a) * (1.0 / GLA_TAU)
    if valid < C:
        lg = jnp.where(lax.broadcasted_iota(jnp.int32, lg.shape, 0) < valid, lg, 0.0)
    low = (row >= col).astype(BF16)
    bc = jnp.zeros_like(lg)
    for part in _split3(lg):
        bc = bc + jnp.dot(low, part, preferred_element_type=F32)

    q = rows(q_ref, F32) * (GLA_DK ** -0.5)
    k = rows(k_ref, F32)
    v = rows(v_ref, BF16)
    s = s_ref[...]

    o_state = jnp.dot((q * jnp.exp(bc)).astype(BF16), s.astype(BF16), preferred_element_type=F32)
    krow = lax.broadcasted_iota(jnp.int32, (C, GLA_DK), 0)
    sub_row = lax.broadcasted_iota(jnp.int32, (GLA_SUB, C), 0)
    sub_col = lax.broadcasted_iota(jnp.int32, (GLA_SUB, C), 1)
    outs = []
    for i in range(C // GLA_SUB):
        r0, r1 = i * GLA_SUB, (i + 1) * GLA_SUB
        if valid <= r0:
            outs.append(o_state[r0:r1])
            continue
        base = bc[r0 - 1:r0] if i > 0 else jnp.zeros((1, GLA_DK), F32)
        qt = (q[r0:r1] * jnp.exp(bc[r0:r1] - base)).astype(BF16)
        kt = jnp.where(krow < r1, k * jnp.exp(base - bc), 0.0).astype(BF16)
        att = lax.dot_general(qt, kt, (((1,), (1,)), ((), ())), preferred_element_type=F32)
        att = jnp.where(sub_col <= sub_row + r0, att, 0.0)
        outs.append(o_state[r0:r1] + jnp.dot(att.astype(BF16), v, preferred_element_type=F32))
    o = jnp.concatenate(outs, axis=0)

    b_end = bc[C - 1:C]
    kd = k * jnp.exp(b_end - bc)
    decay = jnp.broadcast_to(jnp.exp(b_end), (LANES, GLA_DK)).T
    decay = jnp.concatenate([decay] * (GLA_DV // LANES), axis=1)
    s_new = decay * s + jnp.dot(kd.T.astype(BF16), v, preferred_element_type=F32)
    s_ref[...] = s_new

    o = o[:valid]
    on = o * lax.rsqrt(jnp.mean(o * o, axis=-1, keepdims=True) + EPS) * g_ref[...]
    gg = gg_ref[...]
    go_ref[...] = (on * (gg * jax.nn.sigmoid(gg))).astype(go_ref.dtype)


def _gla(z32, z16, small, wa_pad, b_a, g_out, s0, batch, tokens):
    if tokens >= GLA_CHUNKS[-1]:
        chunk = _pick(tokens, GLA_CHUNKS)
        blk, nch = chunk, tokens // chunk
    else:
        chunk, blk, nch = GLA_CHUNKS[-1], tokens, 1
    hb = GLA_HEADS_PER_STEP
    kw, vw = hb * GLA_DK, hb * GLA_DV
    has_init = s0 is not None
    if s0 is None:
        s0 = jnp.zeros((1, hb, GLA_DK, GLA_DV), F32)
        s0_map = lambda b, h, c: (0, 0, 0, 0)
    else:
        s0_map = lambda b, h, c: (b, h, 0, 0)
    qb, kb = COL_GQ // kw, COL_GK // kw
    vb, gb = COL_GV // vw, COL_GG // vw
    go, s_out = pl.pallas_call(
        functools.partial(_gla_kernel, chunk=chunk, valid=blk, has_init=has_init, hb=hb),
        grid=(batch, GLA_HEADS // hb, nch),
        in_specs=[
            pl.BlockSpec((blk, kw), lambda b, h, c: (b * nch + c, qb + h)),
            pl.BlockSpec((blk, kw), lambda b, h, c: (b * nch + c, kb + h)),
            pl.BlockSpec((blk, vw), lambda b, h, c: (b * nch + c, vb + h)),
            pl.BlockSpec((blk, vw), lambda b, h, c: (b * nch + c, gb + h)),
            pl.BlockSpec((blk, LANES), lambda b, h, c: (b * nch + c, 0)),
            pl.BlockSpec((LANES, kw), lambda b, h, c: (0, h)),
            pl.BlockSpec((1, kw), lambda b, h, c: (0, h)),
            pl.BlockSpec((1, GLA_DV), lambda b, h, c: (0, 0)),
            pl.BlockSpec((1, hb, GLA_DK, GLA_DV), s0_map),
        ],
        out_specs=[
            pl.BlockSpec((blk, vw), lambda b, h, c: (b * nch + c, h)),
            pl.BlockSpec((1, hb, GLA_DK, GLA_DV), lambda b, h, c: (b, h, 0, 0)),
        ],
        out_shape=[jax.ShapeDtypeStruct((batch * tokens, GLA_WIDTH), BF16),
                   jax.ShapeDtypeStruct((batch, GLA_HEADS, GLA_DK, GLA_DV), F32)],
        scratch_shapes=[pltpu.VMEM((hb, GLA_DK, GLA_DV), F32)],
        compiler_params=_params("parallel", "parallel", "arbitrary"),
        name="gla",
    )(z32, z32, z16, z32, small, wa_pad, b_a, g_out, s0)
    return go, s_out


def _outproj_kernel(fo_ref, go_ref, wt_ref, wb_ref, x_ref, o_ref):
    o_ref[...] = (x_ref[...] + jnp.dot(fo_ref[...], wt_ref[...], preferred_element_type=F32)
                  + jnp.dot(go_ref[...], wb_ref[...], preferred_element_type=F32))


def _outproj(fo, go, w_out, x):
    t, d = x.shape
    half = fo.shape[1]
    tm = _pick(t, (512, 256, 128))
    tn = _pick(d, (1024, 512))
    return pl.pallas_call(
        _outproj_kernel,
        grid=(d // tn, t // tm),
        in_specs=[
            pl.BlockSpec((tm, half), lambda j, i: (i, 0)),
            pl.BlockSpec((tm, half), lambda j, i: (i, 0)),
            pl.BlockSpec((half, tn), lambda j, i: (0, j)),
            pl.BlockSpec((half, tn), lambda j, i: (1, j)),
            pl.BlockSpec((tm, tn), lambda j, i: (i, j)),
        ],
        out_specs=pl.BlockSpec((tm, tn), lambda j, i: (i, j)),
        out_shape=jax.ShapeDtypeStruct((t, d), F32),
        compiler_params=_params("parallel", "parallel"),
        name="outproj",
    )(fo, go, w_out, w_out, x)


def _mem_kernel(h_ref, gq_ref, wq_ref, mk_ref, mv_ref, wo_ref, gf_ref, h2_ref, n3t_ref, *, per_batch, n_mem):
    h = h_ref[...]
    tm = h.shape[0]
    n2 = (h * lax.rsqrt(jnp.mean(h * h, axis=-1, keepdims=True) + EPS) * gq_ref[...]).astype(BF16)
    q = jnp.dot(n2, wq_ref[...], preferred_element_type=F32)
    scale = MEM_HEAD_DIM ** -0.5
    rows_per = min(per_batch, tm)
    parts = []
    for b in range(tm // rows_per):
        heads = []
        for hd in range(MEM_HEADS):
            cs = slice(hd * MEM_HEAD_DIM, (hd + 1) * MEM_HEAD_DIM)
            qh = q[b * rows_per:(b + 1) * rows_per, cs].astype(BF16)
            kh = mk_ref[b * n_mem:(b + 1) * n_mem, cs].astype(BF16)
            vh = mv_ref[b * n_mem:(b + 1) * n_mem, cs].astype(BF16)
            s = lax.dot_general(qh, kh, (((1,), (1,)), ((), ())), preferred_element_type=F32) * scale
            s = s - jnp.max(s, axis=1, keepdims=True)
            p = jnp.exp(s)
            p = p / jnp.sum(p, axis=1, keepdims=True)
            heads.append(jnp.dot(p.astype(BF16), vh, preferred_element_type=F32))
        parts.append(jnp.concatenate(heads, axis=1))
    o = jnp.concatenate(parts, axis=0).astype(BF16)
    h2 = h + jnp.dot(o, wo_ref[...], preferred_element_type=F32)
    h2_ref[...] = h2
    n3 = h2 * lax.rsqrt(jnp.mean(h2 * h2, axis=-1, keepdims=True) + EPS) * gf_ref[...]
    n3t_ref[...] = n3.T.astype(BF16)


def _mem_block(h1, g_q, w_mq, mk, mv, w_mo, g_ffn, per_batch, n_mem):
    t, d = h1.shape
    tm = _pick(t, (256, 128)) if per_batch >= 256 else LANES
    nb = max(tm // per_batch, 1)
    tiles_per_batch = max(per_batch // tm, 1)
    return pl.pallas_call(
        functools.partial(_mem_kernel, per_batch=per_batch, n_mem=n_mem),
        grid=(t // tm,),
        in_specs=[
            pl.BlockSpec((tm, d), lambda i: (i, 0)),
            pl.BlockSpec((1, d), lambda i: (0, 0)),
            pl.BlockSpec((d, MEM_WIDTH), lambda i: (0, 0)),
            pl.BlockSpec((nb * n_mem, MEM_WIDTH), lambda i: (i // tiles_per_batch, 0)),
            pl.BlockSpec((nb * n_mem, MEM_WIDTH), lambda i: (i // tiles_per_batch, 0)),
            pl.BlockSpec((MEM_WIDTH, d), lambda i: (0, 0)),
            pl.BlockSpec((1, d), lambda i: (0, 0)),
        ],
        out_specs=[pl.BlockSpec((tm, d), lambda i: (i, 0)), pl.BlockSpec((d, tm), lambda i: (0, i))],
        out_shape=[jax.ShapeDtypeStruct((t, d), F32), jax.ShapeDtypeStruct((d, t), BF16)],
        compiler_params=_params("parallel"),
        name="mem_attn",
    )(h1, g_q, w_mq, mk, mv, w_mo, g_ffn)


def _extract_top(vals, key, count, exact):
    cur = vals
    rank = jnp.full(vals.shape, float(count), F32)
    tops = []
    big = jnp.float32(2**30)
    for r in range(count):
        m = jnp.max(cur, axis=0, keepdims=True)
        hit = cur == m
        if exact:
            first = jnp.min(jnp.where(hit, key, big), axis=0, keepdims=True)
            hit = key == first
        rank = jnp.where(hit, float(r), rank)
        cur = jnp.where(hit, -jnp.inf, cur)
        tops.append(m)
    return tops, rank


def _peer_route_kernel(w_ref, n3t_ref, keys_ref, pk2_ref, pk1_ref, qp_sc):
    qp_sc[...] = jnp.dot(w_ref[...], n3t_ref[...], preferred_element_type=F32)
    qd = 2 * PEER_HALF

    def head(hd, carry):
        r0 = pl.multiple_of(hd * qd, qd)
        s1 = jnp.dot(keys_ref[0], qp_sc[pl.ds(r0, PEER_HALF), :].astype(BF16), preferred_element_type=F32)
        s2 = jnp.dot(keys_ref[1], qp_sc[pl.ds(r0 + PEER_HALF, PEER_HALF), :].astype(BF16),
                     preferred_element_type=F32)

        def store(tables):
            rank2, e2, cnt, e1 = tables
            pk2_ref[hd, 0] = rank2.astype(BF16)
            pk2_ref[hd, 1] = e2.astype(BF16)
            pk1_ref[hd, 0] = cnt
            pk1_ref[hd, 1] = e1

        tables, removed = _route_tables(s1, s2, exact=False)
        store(tables)
        ties = jnp.max(jnp.abs(removed - float(3 * PEER_TOPK)))

        @pl.when(ties > 0.0)
        def _():
            store(_route_tables(s1, s2, exact=True)[0])

        return carry

    lax.fori_loop(0, PEER_HEADS, head, 0)


def _route_tables(s1, s2, exact):
    K = PEER_TOPK
    tm = s1.shape[1]
    kiota = lax.broadcasted_iota(jnp.int32, s1.shape, 0).astype(F32)
    a, rank1 = _extract_top(s1, kiota, K, exact)
    b, rank2 = _extract_top(s2, kiota, K, exact)

    half = K // 2
    bmat = jnp.concatenate(b, axis=0)
    jio = lax.broadcasted_iota(jnp.int32, (K, tm), 0).astype(F32)
    jio_h = lax.broadcasted_iota(jnp.int32, (half, tm), 0).astype(F32)
    blocks, keys, spans = [a[0] + bmat], [jio], [(0, K)]
    for i in range(1, half):
        blocks.append(jnp.where(jio_h < float(K // (i + 1)), a[i] + bmat[:half], -jnp.inf))
        keys.append(jio_h + float(i * K))
        spans.append((K + (i - 1) * half, K + i * half))
    blocks.append(jnp.concatenate([a[i] + b[0] for i in range(half, K)], axis=0))
    keys.append((jio_h + float(half)) * float(K))
    tail = K + (half - 1) * half
    spans += [(tail + i, tail + i + 1) for i in range(K - half)]
    cand = jnp.concatenate(blocks, axis=0)
    ckey = jnp.concatenate(keys, axis=0)
    _, crank = _extract_top(cand, ckey, K, exact)
    sel = crank < float(K)
    z = jnp.sum(jnp.where(sel, jnp.exp(cand - (a[0] + b[0])), 0.0), axis=0, keepdims=True)
    self32 = sel.astype(F32)
    counts = [jnp.sum(self32[lo:hi], axis=0, keepdims=True) for lo, hi in spans]

    cnt = jnp.zeros(s1.shape, F32)
    for i in range(K):
        cnt = jnp.where(rank1 == float(i), counts[i], cnt)
    removed = (jnp.sum((rank1 < float(K)).astype(F32), axis=0, keepdims=True)
               + jnp.sum((rank2 < float(K)).astype(F32), axis=0, keepdims=True)
               + jnp.sum(self32, axis=0, keepdims=True))
    return (rank2, jnp.exp(s2 - b[0]), cnt, jnp.exp(s1 - a[0]) / z), removed


def _peer_route(w_pq_t, n3t, keys):
    d, t = n3t.shape
    tm = _pick(t, (512, 256, 128))
    qw = PEER_HEADS * 2 * PEER_HALF
    spec = pl.BlockSpec((PEER_HEADS, 2, PEER_NKEYS, tm), lambda i: (0, 0, 0, i))
    return pl.pallas_call(
        _peer_route_kernel,
        grid=(t // tm,),
        in_specs=[
            pl.BlockSpec((qw, d), lambda i: (0, 0), pipeline_mode=pl.Buffered(1)),
            pl.BlockSpec((d, tm), lambda i: (0, i)),
            pl.BlockSpec((2, PEER_NKEYS, PEER_HALF), lambda i: (0, 0, 0)),
        ],
        out_specs=[spec, spec],
        out_shape=[jax.ShapeDtypeStruct((PEER_HEADS, 2, PEER_NKEYS, t), BF16),
                   jax.ShapeDtypeStruct((PEER_HEADS, 2, PEER_NKEYS, t), F32)],
        scratch_shapes=[pltpu.VMEM((qw, tm), F32)],
        compiler_params=_params("parallel"),
        name="peer_route",
    )(w_pq_t, n3t, keys)


def _gelu_tanh(x):
    inner = x * (0.7978845608028654 + (0.7978845608028654 * 0.044715) * (x * x))
    half = 0.5 * x
    return half + half * jnp.tanh(inner)


def _peer_kernel(n3t_ref, pk2_ref, pk1_ref, u_ref, vt_ref, o_ref, act_sc, *, ec, n_chunks):
    s = pl.program_id(0)
    cur = s % 2
    acc_pair = jnp.maximum(s - 1, 0)
    act_pair = jnp.minimum(s, pl.num_programs(0) - 2)

    @pl.when(s == 0)
    def _():
        act_sc[1] = jnp.zeros(act_sc.shape[1:], act_sc.dtype)

    @pl.when(acc_pair % n_chunks == 0)
    def _():
        o_ref[...] = jnp.zeros_like(o_ref)

    h = jnp.dot(u_ref[...], n3t_ref[...], preferred_element_type=F32)
    o_ref[...] += jnp.dot(vt_ref[0], act_sc[1 - cur], preferred_element_type=F32)
    k1_base = (act_pair % n_chunks) * (ec // PEER_NKEYS)
    for kk in range(ec // PEER_NKEYS):
        rows = slice(kk * PEER_NKEYS, (kk + 1) * PEER_NKEYS)
        gate = jnp.zeros((PEER_NKEYS, h.shape[1]), BF16)
        for hd in range(PEER_HEADS):
            cnt = pk1_ref[hd, 0, pl.ds(k1_base + kk, 1), :].astype(BF16)
            e1 = pk1_ref[hd, 1, pl.ds(k1_base + kk, 1), :].astype(BF16)
            gate = gate + jnp.where(pk2_ref[hd, 0] < cnt, pk2_ref[hd, 1], 0.0) * e1
        act_sc[cur, rows, :] = (_gelu_tanh(h[rows]) * gate.astype(F32)).astype(BF16)


def _peer(n3t, pk2, pk1, u, vt):
    d, t = n3t.shape
    n_chunks, _, ec = vt.shape
    tm = _pick(t, (512, 256, 128))
    once = pl.Buffered(1)
    n_pairs = (t // tm) * n_chunks
    act_pair = lambda s: jnp.minimum(s, n_pairs - 1)
    acc_pair = lambda s: jnp.maximum(s - 1, 0)
    return pl.pallas_call(
        functools.partial(_peer_kernel, ec=ec, n_chunks=n_chunks),
        grid=(n_pairs + 1,),
        in_specs=[
            pl.BlockSpec((d, tm), lambda s: (0, act_pair(s) // n_chunks), pipeline_mode=once),
            pl.BlockSpec((PEER_HEADS, 2, PEER_NKEYS, tm), lambda s: (0, 0, 0, act_pair(s) // n_chunks),
                         pipeline_mode=once),
            pl.BlockSpec((PEER_HEADS, 2, PEER_NKEYS, tm), lambda s: (0, 0, 0, act_pair(s) // n_chunks),
                         pipeline_mode=once),
            pl.BlockSpec((ec, d), lambda s: (act_pair(s) % n_chunks, 0)),
            pl.BlockSpec((1, d, ec), lambda s: (acc_pair(s) % n_chunks, 0, 0)),
        ],
        out_specs=pl.BlockSpec((d, tm), lambda s: (0, acc_pair(s) // n_chunks)),
        out_shape=jax.ShapeDtypeStruct((d, t), F32),
        scratch_shapes=[pltpu.VMEM((2, ec, tm), BF16)],
        compiler_params=_params("arbitrary"),
        name="peer_experts",
    )(n3t, pk2, pk1, u, vt)


def _final_kernel(h_ref, pt_ref, g_ref, o_ref):
    x = h_ref[...] + pt_ref[...].T
    y = x * lax.rsqrt(jnp.mean(x * x, axis=-1, keepdims=True) + EPS)
    o_ref[...] = y * g_ref[...]


def _final(h2, peer_t, g):
    t, d = h2.shape
    tm = _pick(t, (256, 128))
    return pl.pallas_call(
        _final_kernel,
        grid=(t // tm,),
        in_specs=[pl.BlockSpec((tm, d), lambda i: (i, 0)), pl.BlockSpec((d, tm), lambda i: (0, i)),
                  pl.BlockSpec((1, d), lambda i: (0, 0))],
        out_specs=pl.BlockSpec((tm, d), lambda i: (i, 0)),
        out_shape=jax.ShapeDtypeStruct((t, d), F32),
        compiler_params=_params("parallel"),
        name="final_norm",
    )(h2, peer_t, g)


def _layer(x, w, mem_k, mem_v, batch, tokens, n_mem, past):
    n, q16 = _norm_mm(x, w["norm_mix"], w["w_fox"], FOX_WIDTH, FOX_QSCALE)
    fk, k16 = _mm(n, w["w_fox"], (F32, BF16), COL_FK, FOX_WIDTH, w_rows=True)
    fv, v16 = _mm(n, w["w_fox"], (F32, BF16), COL_FV, FOX_WIDTH, w_rows=True)
    z32, z16 = _mm(n, w["w_gla"], (F32, BF16), w_rows=True)
    small = _small_proj(n, w["w_small"], w["b_fgate"])
    logf = small[:, :FOX_HEADS]

    if past is None:
        lb = _pick(tokens, (512, 256, 128))
        c = _cumsum_t(small.reshape(batch, tokens, LANES), lb)
        tq = _pick(tokens, (512, 256, 128))
        c4 = c.reshape(batch * LANES, tokens // tq, 1, tq)
        fo = _fox_prompt(q16, k16, v16, c4, batch, tokens)
        go, gla_state = _gla(z32, z16, small, w["wa_pad"], w["b_gla_a"], w["norm_gla_out"], None, batch, tokens)
    else:
        ck, cv, clogf, s0 = past
        plen = ck.shape[1]
        lf = jnp.concatenate([
            jnp.pad(clogf.astype(F32), ((0, 0), (0, 0), (0, LANES - FOX_HEADS))),
            small.reshape(batch, tokens, LANES),
            jnp.zeros((batch, LANES - tokens, LANES), F32)], axis=1)
        c = _cumsum_t(lf, LANES)
        c3 = c.reshape(batch * LANES, 1, plen + LANES)
        fo = _fox_sample(q16, k16, v16, ck.reshape(batch, plen * FOX_HEADS, FOX_HEAD_DIM),
                         cv.reshape(batch, plen * FOX_HEADS, FOX_HEAD_DIM), c3, batch, tokens, plen)
        go, gla_state = _gla(z32, z16, small, w["wa_pad"], w["b_gla_a"], w["norm_gla_out"], s0.astype(F32),
                             batch, tokens)

    h1 = _outproj(fo, go, w["w_out"], x)
    h2, n3t = _mem_block(h1, w["norm_mem_q"], w["w_mq"], mem_k, mem_v, w["w_mo"], w["norm_ffn"], tokens, n_mem)
    pk2, pk1 = _peer_route(w["w_pq_t"], n3t, w["sub_keys"])
    peer_t = _peer(n3t, pk2, pk1, w["expert_u"], w["expert_v_t"])
    return h2, peer_t, fk, fv, logf, gla_state


def kernel(x_prompt, x_sample, cache_fox_k, cache_fox_v, cache_fox_logf, state_gla, cache_mem_k, cache_mem_v, mem_prompt, norm_mix, w_in, b_fgate, w_gla_a2, b_gla_a, norm_gla_out, w_out, norm_mem_q, norm_mem_kv, w_mq, w_mk, w_mv, w_mo, norm_ffn, w_pq, sub_keys, expert_u, expert_v, norm_final):
    depth = w_in.shape[0]
    assert depth == 1, "one trunk layer"
    bp, sp, d = x_prompt.shape
    bs, ss, _ = x_sample.shape
    n_mem = mem_prompt.shape[1]
    l = 0

    wt = jnp.swapaxes(w_in, 1, 2)[l]
    o_ff = 3 * FOX_WIDTH
    o_g = o_ff + FOX_HEADS
    o_ga = o_g + GLA_PROJ_WIDTH
    w = {
        "norm_mix": norm_mix[l],
        "w_fox": wt[:o_ff].astype(BF16),
        "w_gla": wt[o_g:o_ga].astype(BF16),
        "w_small": jnp.concatenate([wt[o_ff:o_g], wt[o_ga:],
                                    jnp.zeros((LANES - FOX_HEADS - GLA_GATE_RANK, d), F32)], axis=0).astype(BF16),
        "b_fgate": jnp.pad(b_fgate[l], (0, LANES - FOX_HEADS)).reshape(1, LANES),
        "wa_pad": jnp.pad(w_gla_a2[l], ((FOX_HEADS, LANES - FOX_HEADS - GLA_GATE_RANK), (0, 0))).astype(BF16),
        "b_gla_a": b_gla_a[l].reshape(1, GLA_KEY_WIDTH),
        "norm_gla_out": norm_gla_out[l].reshape(1, GLA_DV),
        "w_out": w_out[l].astype(BF16),
        "norm_mem_q": norm_mem_q[l].reshape(1, d),
        "w_mq": w_mq[l].astype(BF16),
        "w_mo": w_mo[l].astype(BF16),
        "norm_ffn": norm_ffn[l].reshape(1, d),
        "w_pq_t": w_pq[l].T.astype(BF16),
        "sub_keys": sub_keys[l].astype(BF16),
        "expert_u": expert_u[l].astype(BF16),
        "expert_v_t": expert_v[l].reshape(-1, PEER_EXPERT_CHUNK, d).transpose(0, 2, 1).astype(BF16),
    }

    m = _rmsnorm(mem_prompt.reshape(bp * n_mem, d), norm_mem_kv[l])
    (mk,) = _mm(m, w_mk[l].astype(BF16), (F32,))
    (mv,) = _mm(m, w_mv[l].astype(BF16), (F32,))

    hp, pp, fkp, fvp, lfp, gsp = _layer(x_prompt.reshape(bp * sp, d), w, mk, mv, bp, sp, n_mem, None)
    past = (cache_fox_k[l], cache_fox_v[l], cache_fox_logf[l], state_gla[l])
    hs, ps, fks, fvs, lfs, gss = _layer(x_sample.reshape(bs * ss, d), w,
                                        cache_mem_k[l].reshape(bs * n_mem, MEM_WIDTH),
                                        cache_mem_v[l].reshape(bs * n_mem, MEM_WIDTH), bs, ss, n_mem, past)

    g_fin = norm_final.reshape(1, d)
    y_prompt = _final(hp, pp, g_fin).reshape(bp, sp, d)
    y_sample = _final(hs, ps, g_fin).reshape(bs, ss, d)
    hshape_p = (1, bp, sp, FOX_HEADS, FOX_HEAD_DIM)
    hshape_s = (1, bs, ss, FOX_HEADS, FOX_HEAD_DIM)
    return (y_prompt, y_sample,
            fkp.reshape(hshape_p), fvp.reshape(hshape_p), lfp.reshape(1, bp, sp, FOX_HEADS), gsp[None],
            mk.reshape(1, bp, n_mem, MEM_HEADS, MEM_HEAD_DIM), mv.reshape(1, bp, n_mem, MEM_HEADS, MEM_HEAD_DIM),
            fks.reshape(hshape_s), fvs.reshape(hshape_s), lfs.reshape(1, bs, ss, FOX_HEADS), gss[None])
```
